```python
import math
import jax
import jax.numpy as jnp
from jax import lax
import numpy as np

D_MODEL = 4096
BATCH = 4
SEQ = 4096
DEPTH = 2

CTX_LEN = 256
GRID_W = 64
EPS = 1e-6
CHUNK = 64
SHORT_CONV = 3
N_MOD = 6
NEG_BIG = -1e30
LB_FLOOR = 1e-30

HG_WIDTH = D_MODEL // 4
HG_HEADS = 8
HG_DK = HG_WIDTH // HG_HEADS
HG_DV = HG_WIDTH // HG_HEADS
ML_WIDTH = D_MODEL // 2
ML_HEADS = 8
ML_DH = ML_WIDTH // ML_HEADS
HY_WIDTH = D_MODEL // 4
HY_ORDER = 2
HY_POS_DIM = 33
HY_BANDS = (HY_POS_DIM - 1) // 2
HY_FFN = 64
HY_MIN_DECAY = math.log(1e-2) / 1.5
HY_MAX_DECAY = math.log(1e-2) / 0.3

N_BRANCH = 3
BRANCH_WIDTH = HG_WIDTH + ML_WIDTH + HY_WIDTH
STATE_SIZES = (HG_WIDTH, HG_WIDTH, HG_WIDTH, ML_WIDTH, ML_WIDTH, 4 * ML_HEADS)
REST_SIZES = (HG_WIDTH, HG_WIDTH, ML_WIDTH, ML_WIDTH, 3 * HY_WIDTH, N_BRANCH * D_MODEL)
N_STATE_COLS = 3 * HG_WIDTH + 2 * ML_WIDTH + 4 * ML_HEADS
N_IN = N_STATE_COLS + 2 * HG_WIDTH + 2 * ML_WIDTH + 3 * HY_WIDTH + N_BRANCH * D_MODEL

DENSE_FF = 11008
N_EXPERTS = 8
TOP_K = 2
EXPERT_FF = 2048
N_DENSE = (DEPTH + 1) // 2
N_MOE = DEPTH // 2

kernel_name = "hybrid_gated_hgrn2_mlstm_hyena_moe_dit"


def _split(p, sizes):
    out, start = [], 0
    for s in sizes:
        out.append(p[..., start:start + s])
        start += s
    return out


def _rmsnorm(x, w):
    xf = x.astype(jnp.float32)
    y = xf * lax.rsqrt(jnp.mean(xf * xf, axis=-1, keepdims=True) + EPS)
    return (y * w.astype(jnp.float32)).astype(x.dtype)


def _head_rmsnorm(y, w, n_heads):
    b, L, W = y.shape
    yh = y.reshape(b, L, n_heads, W // n_heads)
    yh = yh * lax.rsqrt(jnp.mean(yh * yh, axis=-1, keepdims=True) + EPS)
    return yh.reshape(b, L, W) * w


def _modulate(x, w, shift, scale):
    return _rmsnorm(x, w) * (1.0 + scale) + shift


def _ada(cond, w, b, n_chunks):
    m = jax.nn.silu(cond) @ w[:, :n_chunks * D_MODEL] + b[:n_chunks * D_MODEL]
    return jnp.split(m, n_chunks, axis=-1)


def _heads(a, n_heads):
    b, L, W = a.shape
    return a.reshape(b, L, n_heads, W // n_heads).transpose(0, 2, 1, 3)


def _merge_heads(a):
    b, h, L, d = a.shape
    return a.transpose(0, 2, 1, 3).reshape(b, L, h * d)


def _short_conv(u, w, b, n_rows):
    bsz, L, C = u.shape
    seg = L // n_rows
    pad = SHORT_CONV // 2
    us = jnp.pad(u.reshape(bsz, n_rows, seg, C), ((0, 0), (0, 0), (pad, pad), (0, 0)))
    y = b
    for j in range(SHORT_CONV):
        y = y + w[j] * us[:, :, j:j + seg]
    return y.reshape(bsz, L, C)


def _to_chunks(a):
    b, h, L = a.shape[:3]
    a = a.reshape((b, h, L // CHUNK, CHUNK) + a.shape[3:])
    return jnp.moveaxis(a, 2, 0)


def _from_chunks(a):
    a = jnp.moveaxis(a, 0, 2)
    return a.reshape(a.shape[:2] + (-1,) + a.shape[4:])


def _rev(a):
    return None if a is None else jnp.flip(a, axis=2)


def _hgrn2_scan(q, k, v, g, s0):
    mask = jnp.tril(jnp.ones((CHUNK, CHUNK), dtype=bool))
    with_out = q is not None

    def step(s, inp):
        kc, vc, gc = inp[:3]
        b = jnp.cumsum(gc, axis=2)
        b_end = b[:, :, -1]
        s_new = jnp.exp(b_end)[..., None] * s + jnp.einsum(
            'bhck,bhcv->bhkv', kc * jnp.exp(b_end[:, :, None] - b), vc)
        if not with_out:
            return s_new, None
        qc = inp[3]
        m5 = mask[:, :, None]
        rel = jnp.where(m5, b[:, :, :, None] - b[:, :, None], 0.0)
        dec = jnp.where(m5, jnp.exp(rel), 0.0)
        att = jnp.einsum('bhtk,bhtsk,bhsk->bhts', qc, dec, kc)
        o = jnp.einsum('bhtk,bhkv->bhtv', qc * jnp.exp(b), s) + jnp.einsum('bhts,bhsv->bhtv', att, vc)
        return s_new, o

    xs = (_to_chunks(k), _to_chunks(v), _to_chunks(g)) + ((_to_chunks(q),) if with_out else ())
    s_fin, o = lax.scan(step, s0, xs)
    return s_fin, (_from_chunks(o) if with_out else None)


def _mlstm_scan(q, k, v, logi, logf, state0):
    mask = jnp.tril(jnp.ones((CHUNK, CHUNK), dtype=bool))
    with_out = q is not None

    def step(carry, inp):
        cm, nm, m = carry
        kc, vc, ic, fc = inp[:4]
        b = jnp.cumsum(fc, axis=-1)
        b_end = b[..., -1]
        log_w = b_end[..., None] - b + ic
        m_new = jnp.maximum(b_end + m, jnp.max(log_w, axis=-1))
        decay = jnp.exp(b_end + m - m_new)
        w = jnp.exp(log_w - m_new[..., None])
        c_new = decay[..., None, None] * cm + jnp.einsum('bhs,bhsk,bhsv->bhkv', w, kc, vc)
        n_new = decay[..., None] * nm + jnp.einsum('bhs,bhsk->bhk', w, kc)
        if not with_out:
            return (c_new, n_new, m_new), None
        qc = inp[4]
        log_d_raw = b[..., :, None] - b[..., None, :] + ic[..., None, :]
        log_prev = b + m[..., None]
        m_t = jnp.maximum(log_prev, jnp.max(jnp.where(mask, log_d_raw, NEG_BIG), axis=-1))
        prev = jnp.exp(log_prev - m_t)
        dmat = jnp.where(mask, jnp.exp(jnp.where(mask, log_d_raw - m_t[..., None], 0.0)), 0.0)
        s = jnp.einsum('bhtk,bhsk->bhts', qc, kc) * dmat
        num = prev[..., None] * jnp.einsum('bhtk,bhkv->bhtv', qc, cm) + jnp.einsum('bhts,bhsv->bhtv', s, vc)
        den = prev * jnp.einsum('bhtk,bhk->bht', qc, nm) + jnp.sum(s, axis=-1)
        h = num / jnp.maximum(jnp.abs(den), jnp.exp(-m_t))[..., None]
        return (c_new, n_new, m_new), h

    xs = (_to_chunks(k), _to_chunks(v), _to_chunks(logi), _to_chunks(logf)) + ((_to_chunks(q),) if with_out else ())
    fin, h = lax.scan(step, state0, xs)
    return fin, (_from_chunks(h) if with_out else None)


def _hyena_kernel_fft(L, w1, b1, w2, b2, w3, freq):
    f32 = jnp.float32
    pos = jnp.arange(L, dtype=f32)
    t = pos / (L - 1)
    ang = (2.0 * math.pi / L) * pos
    bands = jnp.linspace(1e-4, HY_BANDS - 1, HY_BANDS, dtype=f32)
    z = jnp.concatenate([t[:, None], jnp.cos(ang[:, None] * bands), jnp.sin(ang[:, None] * bands)], axis=-1)
    a = jnp.sin(freq[0] * (z @ w1 + b1))
    a = jnp.sin(freq[1] * (a @ w2 + b2))
    filt = (a @ w3).reshape(L, 2, HY_ORDER, HY_WIDTH)
    deltas = jnp.abs(jnp.linspace(HY_MIN_DECAY, HY_MAX_DECAY, HY_WIDTH, dtype=f32))
    filt = filt * jnp.exp(-t[:, None] * deltas)[:, None, None, :]
    kern = jnp.concatenate([filt[:, 0], jnp.zeros((1, HY_ORDER, HY_WIDTH), f32), filt[:L - 1, 1][::-1]], axis=0)
    kern = kern / (jnp.sum(jnp.abs(kern), axis=0, keepdims=True) + EPS)
    return jnp.fft.rfft(kern, axis=0)


def _hyena(p, lp, n_rows):
    L = p.shape[1]
    u = _short_conv(p, lp["hy_conv_w"], lp["hy_conv_b"], n_rows)
    v, x1, x2 = jnp.split(u, 3, axis=-1)
    kf = _hyena_kernel_fft(L, lp["hy_w1"], lp["hy_b1"], lp["hy_w2"], lp["hy_b2"], lp["hy_w3"], lp["hy_freq"])
    z = v
    for o, gate in enumerate((x1, x2)):
        conv = jnp.fft.irfft(jnp.fft.rfft(z, n=2 * L, axis=1) * kf[:, o], n=2 * L, axis=1)[:, :L]
        z = gate * (conv + lp["hy_skip"][o] * z)
    return z


def _zero_states(bsz):
    f32 = jnp.float32
    hg = jnp.zeros((bsz, HG_HEADS, HG_DK, HG_DV), f32)
    ml = (jnp.zeros((bsz, ML_HEADS, ML_DH, ML_DH), f32), jnp.zeros((bsz, ML_HEADS, ML_DH), f32),
          jnp.zeros((bsz, ML_HEADS), f32))
    return ((hg, hg), (ml, ml))


def _mixer(x, shift, scale, lp, lb, n_rows, init, with_out):
    f32 = jnp.float32
    h = _modulate(x, lp["norm1"], shift, scale)
    w_in = lp["w_in"]
    if with_out:
        p = h @ w_in
        p_state, p_rest = p[..., :N_STATE_COLS], p[..., N_STATE_COLS:]
    else:
        p_state = h @ w_in[:, :N_STATE_COLS]
    hf_pre, hb_pre, hg_i, ml_k, ml_v, ml_g = _split(p_state.astype(f32), STATE_SIZES)

    lb = lb.astype(f32)
    log_lb, log_1mlb = jnp.log(jnp.maximum(lb, LB_FLOOR)), jnp.log1p(-lb)
    hg_k, hg_g = [], []
    for d, pre in enumerate((hf_pre, hb_pre)):
        hg_g.append(_heads(jnp.logaddexp(log_lb[d], log_1mlb[d] + jax.nn.log_sigmoid(pre)), HG_HEADS))
        hg_k.append(_heads((1.0 - lb[d]) * jax.nn.sigmoid(-pre), HG_HEADS))
    hg_v = _heads(hg_i, HG_HEADS)

    conv_w, conv_b = lp["ml_conv_w"], lp["ml_conv_b"]
    ml_k = _heads(jax.nn.silu(_short_conv(ml_k, conv_w[:, :ML_WIDTH], conv_b[:ML_WIDTH], n_rows)) * ML_DH ** -0.5, ML_HEADS)
    ml_v = _heads(ml_v, ML_HEADS)
    gt = (ml_g + lp["ml_gate_b"]).reshape(ml_g.shape[:2] + (2, 2, ML_HEADS)).transpose(2, 3, 0, 4, 1)
    logi = (gt[0, 0], gt[1, 0])
    logf = (jax.nn.log_sigmoid(gt[0, 1]), jax.nn.log_sigmoid(gt[1, 1]))

    hg_q = ml_q = None
    if with_out:
        hg_q_pre, hg_og, ml_q_pre, ml_og, hy_p, merge_p = _split(p_rest.astype(f32), REST_SIZES)
        hg_q = _heads(jax.nn.silu(hg_q_pre), HG_HEADS)
        ml_q = _heads(jax.nn.silu(_short_conv(ml_q_pre, conv_w[:, ML_WIDTH:], conv_b[ML_WIDTH:], n_rows)), ML_HEADS)

    hg_o, ml_o, fin_hg, fin_ml = [], [], [], []
    for d in range(2):
        tr = (lambda a: a) if d == 0 else _rev
        s_hg, o_hg = _hgrn2_scan(tr(hg_q), tr(hg_k[d]), tr(hg_v), tr(hg_g[d]), init[0][d])
        s_ml, o_ml = _mlstm_scan(tr(ml_q), tr(ml_k), tr(ml_v), tr(logi[d]), tr(logf[d]), init[1][d])
        fin_hg.append(s_hg)
        fin_ml.append(s_ml)
        if with_out:
            hg_o.append(tr(o_hg))
            ml_o.append(tr(o_ml))
    finals = ((fin_hg[0], fin_hg[1]), (fin_ml[0], fin_ml[1]))
    if not with_out:
        return None, finals

    y_hg = _head_rmsnorm(_merge_heads(hg_o[0] + hg_o[1]), lp["hg_norm_w"], HG_HEADS) * jax.nn.silu(hg_og)
    y_ml = _head_rmsnorm(_merge_heads(ml_o[0] + ml_o[1]), lp["ml_norm_w"], ML_HEADS) * jax.nn.sigmoid(ml_og)
    y_hy = _hyena(hy_p, lp, n_rows)

    g_hg, g_ml, g_hy = jnp.split(jax.nn.sigmoid(merge_p), N_BRANCH, axis=-1)
    wb = lp["w_branch"]
    merged = (g_hg * (y_hg @ wb[:HG_WIDTH])
              + g_ml * (y_ml @ wb[HG_WIDTH:HG_WIDTH + ML_WIDTH])
              + g_hy * (y_hy @ wb[HG_WIDTH + ML_WIDTH:]))
    return merged @ lp["w_out"], finals


def _swiglu(h, wg, wu, wd):
    return (jax.nn.silu(h @ wg) * (h @ wu)) @ wd


def _moe(h, router, wg, wu, wd):
    logits = (h @ router).astype(jnp.float32)
    top_val, top_idx = lax.top_k(logits, TOP_K)
    top_p = jax.nn.softmax(top_val, axis=-1)
    combine = jnp.einsum('...k,...ke->...e', top_p, jax.nn.one_hot(top_idx, N_EXPERTS, dtype=jnp.float32))
    y = jnp.zeros(h.shape, jnp.float32)
    for e in range(N_EXPERTS):
        y = y + combine[..., e:e + 1] * _swiglu(h, wg[e], wu[e], wd[e])
    return y


def _channel_mixer(h, l, dense_w_gate, dense_w_up, dense_w_down, moe_router, moe_w_gate, moe_w_up, moe_w_down):
    if l % 2 == 0:
        i = l // 2
        return _swiglu(h, dense_w_gate[i], dense_w_up[i], dense_w_down[i])
    i = l // 2
    return _moe(h, moe_router[i], moe_w_gate[i], moe_w_up[i], moe_w_down[i])


def setup_inputs(seed: int = 0) -> dict:
    key = jax.random.key(seed)
    ks = iter(jax.random.split(key, 64))
    f32 = jnp.float32
    D = D_MODEL

    def normal(shape, scale):
        return jax.random.normal(next(ks), shape, f32) * scale

    ig = normal((DEPTH, 2, 1, ML_HEADS), 0.1)
    fg = jnp.linspace(3.0, 6.0, ML_HEADS, dtype=f32) + normal((DEPTH, 2, 1, ML_HEADS), 0.1)
    w_branch = jnp.concatenate([normal((DEPTH, HG_WIDTH, D), HG_WIDTH ** -0.5),
                                normal((DEPTH, ML_WIDTH, D), ML_WIDTH ** -0.5),
                                normal((DEPTH, HY_WIDTH, D), HY_WIDTH ** -0.5)], axis=1)
    return {
        "x": normal((BATCH, SEQ, D), 1.0),
        "c": normal((BATCH, D), 1.0),
        "ctx": normal((BATCH, CTX_LEN, D), 1.0),
        "c_ctx": normal((D,), 1.0),
        "norm1_w": 1.0 + normal((DEPTH, D), 0.02),
        "norm2_w": 1.0 + normal((DEPTH, D), 0.02),
        "ada_w": normal((DEPTH, D, N_MOD * D), D ** -0.5),
        "ada_b": normal((DEPTH, N_MOD * D), 0.02),
        "w_in": normal((DEPTH, D, N_IN), D ** -0.5),
        "hg_lb_logits": normal((DEPTH, 2, HG_WIDTH), 0.5),
        "hg_norm_w": 1.0 + normal((DEPTH, HG_WIDTH), 0.02),
        "ml_conv_w": normal((DEPTH, SHORT_CONV, 2 * ML_WIDTH), SHORT_CONV ** -0.5),
        "ml_conv_b": normal((DEPTH, 2 * ML_WIDTH), 0.02),
        "ml_gate_b": jnp.concatenate([ig, fg], axis=2).reshape(DEPTH, 4 * ML_HEADS),
        "ml_norm_w": 1.0 + normal((DEPTH, ML_WIDTH), 0.02),
        "hy_conv_w": normal((DEPTH, SHORT_CONV, 3 * HY_WIDTH), SHORT_CONV ** -0.5),
        "hy_conv_b": normal((DEPTH, 3 * HY_WIDTH), 0.02),
        "hy_w1": normal((DEPTH, HY_POS_DIM, HY_FFN), HY_POS_DIM ** -0.5),
        "hy_b1": normal((DEPTH, HY_FFN), 0.1),
        "hy_w2": normal((DEPTH, HY_FFN, HY_FFN), HY_FFN ** -0.5),
        "hy_b2": normal((DEPTH, HY_FFN), 0.1),
        "hy_w3": normal((DEPTH, HY_FFN, 2 * HY_ORDER * HY_WIDTH), HY_FFN ** -0.5),
        "hy_freq": 1.0 + normal((DEPTH, 2, HY_FFN), 0.02),
        "hy_skip": normal((DEPTH, HY_ORDER, HY_WIDTH), 0.5),
        "w_branch": w_branch,
        "w_out": normal((DEPTH, D, D), D ** -0.5),
        "dense_w_gate": normal((N_DENSE, D, DENSE_FF), D ** -0.5),
        "dense_w_up": normal((N_DENSE, D, DENSE_FF), D ** -0.5),
        "dense_w_down": normal((N_DENSE, DENSE_FF, D), DENSE_FF ** -0.5),
        "moe_router": normal((N_MOE, D, N_EXPERTS), D ** -0.5),
        "moe_w_gate": normal((N_MOE, N_EXPERTS, D, EXPERT_FF), D ** -0.5),
        "moe_w_up": normal((N_MOE, N_EXPERTS, D, EXPERT_FF), D ** -0.5),
        "moe_w_down": normal((N_MOE, N_EXPERTS, EXPERT_FF, D), EXPERT_FF ** -0.5),
        "final_norm_w": 1.0 + normal((D,), 0.02),
    }


def reference(x, c, ctx, c_ctx, norm1_w, norm2_w, ada_w, ada_b, w_in, hg_lb_logits, hg_norm_w,
              ml_conv_w, ml_conv_b, ml_gate_b, ml_norm_w, hy_conv_w, hy_conv_b, hy_w1, hy_b1,
              hy_w2, hy_b2, hy_w3, hy_freq, hy_skip, w_branch, w_out, dense_w_gate, dense_w_up,
              dense_w_down, moe_router, moe_w_gate, moe_w_up, moe_w_down, final_norm_w):
    rows = x.shape[1] // GRID_W
    lb_w = jax.nn.softmax(hg_lb_logits.astype(jnp.float32), axis=0)
    lower_bounds = jnp.cumsum(lb_w, axis=0) - lb_w[0]
    c_lat = c[:, None, :]
    c_cx = c_ctx[None, None, :]
    ffn_w = (dense_w_gate, dense_w_up, dense_w_down, moe_router, moe_w_gate, moe_w_up, moe_w_down)
    for l in range(DEPTH):
        lp = {
            "norm1": norm1_w[l], "w_in": w_in[l], "hg_norm_w": hg_norm_w[l],
            "ml_conv_w": ml_conv_w[l], "ml_conv_b": ml_conv_b[l], "ml_gate_b": ml_gate_b[l],
            "ml_norm_w": ml_norm_w[l], "hy_conv_w": hy_conv_w[l], "hy_conv_b": hy_conv_b[l],
            "hy_w1": hy_w1[l], "hy_b1": hy_b1[l], "hy_w2": hy_w2[l], "hy_b2": hy_b2[l],
            "hy_w3": hy_w3[l], "hy_freq": hy_freq[l], "hy_skip": hy_skip[l],
            "w_branch": w_branch[l], "w_out": w_out[l],
        }
        zero = _zero_states(ctx.shape[0])
        if l == DEPTH - 1:
            sh_c, sc_c = _ada(c_cx, ada_w[l], ada_b[l], 2)
            _, ctx_fin = _mixer(ctx, sh_c, sc_c, lp, lower_bounds[l], 1, zero, False)
        else:
            sh_c, sc_c, g_c, sh2_c, sc2_c, g2_c = _ada(c_cx, ada_w[l], ada_b[l], N_MOD)
            ctx_mix, ctx_fin = _mixer(ctx, sh_c, sc_c, lp, lower_bounds[l], 1, zero, True)
            ctx = ctx + (g_c * ctx_mix).astype(ctx.dtype)
            ctx = ctx + (g2_c * _channel_mixer(_modulate(ctx, norm2_w[l], sh2_c, sc2_c), l, *ffn_w)).astype(ctx.dtype)
        sh, sc, g, sh2, sc2, g2 = _ada(c_lat, ada_w[l], ada_b[l], N_MOD)
        mix, _ = _mixer(x, sh, sc, lp, lower_bounds[l], rows, ctx_fin, True)
        x = x + (g * mix).astype(x.dtype)
        x = x + (g2 * _channel_mixer(_modulate(x, norm2_w[l], sh2, sc2), l, *ffn_w)).astype(x.dtype)
    return _rmsnorm(x, final_norm_w)
```

```python
import functools
import math

import jax
import jax.numpy as jnp
from jax import lax
from jax.experimental import pallas as pl
from jax.experimental.pallas import tpu as pltpu

D_MODEL = 4096
DEPTH = 2
GRID_W = 64
EPS = 1e-6
CHUNK = 64
SHORT_CONV = 3
N_MOD = 6
NEG_BIG = -1e30
LB_FLOOR = 1e-30

HG_WIDTH = D_MODEL // 4
HG_HEADS = 8
ML_WIDTH = D_MODEL // 2
ML_HEADS = 8
ML_DH = ML_WIDTH // ML_HEADS
HY_WIDTH = D_MODEL // 4
HY_ORDER = 2
HY_POS_DIM = 33
HY_BANDS = (HY_POS_DIM - 1) // 2
HY_MIN_DECAY = math.log(1e-2) / 1.5
HY_MAX_DECAY = math.log(1e-2) / 0.3
N_BRANCH = 3

N_GATE_COLS = 4 * ML_HEADS
N_STATE_MAIN = 3 * HG_WIDTH + 2 * ML_WIDTH
N_STATE_COLS = N_STATE_MAIN + N_GATE_COLS
N_MAIN = N_STATE_MAIN + 2 * HG_WIDTH + 2 * ML_WIDTH + 3 * HY_WIDTH + N_BRANCH * D_MODEL

DENSE_FF = 11008
N_EXPERTS = 8
TOP_K = 2
EXPERT_FF = 2048

V7X_LANES = 128
V7X_VMEM_BYTES = 64 * 1024 * 1024
DENSE_FF_PAD = 11264

_F32 = jnp.float32
_BF16 = jnp.bfloat16


def _vmem_limit(*block_bytes):
    need = int(sum(block_bytes)) + (4 << 20)
    return min(max(need, 16 << 20), V7X_VMEM_BYTES - (6 << 20))


def _mm_kernel(*refs, nk, has_res):
    if has_res:
        x_ref, w_ref, res_ref, gate_ref = refs[:4]
        rest = refs[4:]
    else:
        x_ref, w_ref = refs[:2]
        rest = refs[2:]
    o_ref = rest[0]

    def finish(acc):
        if has_res:
            acc = res_ref[...] + gate_ref[0] * acc
        o_ref[...] = acc.astype(o_ref.dtype)

    part = jnp.dot(x_ref[...], w_ref[...], preferred_element_type=_F32)
    if nk == 1:
        finish(part)
        return
    acc_ref = rest[1]
    k = pl.program_id(2)

    @pl.when(k == 0)
    def _():
        acc_ref[...] = part

    @pl.when(k > 0)
    def _():
        acc_ref[...] += part

    @pl.when(k == nk - 1)
    def _():
        finish(acc_ref[...])


def _matmul(x, w, *, out_dtype, bm=1024, bn=1024, bk=None, res=None, gate=None, rows_per_gate=None):
    m, kdim = x.shape
    n = w.shape[1]
    bm, bn = min(bm, m), min(bn, n)
    bk = kdim if bk is None else bk
    assert m % bm == 0 and n % bn == 0 and kdim % bk == 0, (x.shape, w.shape, bm, bn, bk)
    nk = kdim // bk
    has_res = res is not None
    in_specs = [pl.BlockSpec((bm, bk), lambda i, j, k: (i, k)),
                pl.BlockSpec((bk, bn), lambda i, j, k: (k, j))]
    args = [x, w]
    osz = jnp.dtype(out_dtype).itemsize
    vm = [2 * bm * bk * 2, 2 * bk * bn * 2, 2 * bm * bn * osz, 2 * bm * bn * 4]
    if has_res:
        blocks_per_gate = rows_per_gate // bm
        assert rows_per_gate % bm == 0
        in_specs += [pl.BlockSpec((bm, bn), lambda i, j, k: (i, j)),
                     pl.BlockSpec((1, 1, bn), lambda i, j, k: (i // blocks_per_gate, 0, j))]
        args += [res, gate]
        vm.append(2 * bm * bn * 4)
    scratch = [pltpu.VMEM((bm, bn), _F32)] if nk > 1 else []
    return pl.pallas_call(
        functools.partial(_mm_kernel, nk=nk, has_res=has_res),
        grid=(m // bm, n // bn, nk),
        in_specs=in_specs,
        out_specs=pl.BlockSpec((bm, bn), lambda i, j, k: (i, j)),
        out_shape=jax.ShapeDtypeStruct((m, n), out_dtype),
        scratch_shapes=scratch,
        compiler_params=pltpu.CompilerParams(
            dimension_semantics=("parallel", "parallel", "arbitrary"),
            vmem_limit_bytes=_vmem_limit(*vm)),
        name="matmul",
    )(*args)


def _swiglu_kernel(*refs, has_scale, cols_per_expert, bn):
    if has_scale:
        x_ref, wg_ref, wu_ref, comb_ref, o_ref = refs
    else:
        x_ref, wg_ref, wu_ref, o_ref = refs
    x = x_ref[...]
    g = jnp.dot(x, wg_ref[...], preferred_element_type=_F32)
    u = jnp.dot(x, wu_ref[...], preferred_element_type=_F32)
    hid = g * jax.nn.sigmoid(g) * u
    if has_scale:
        expert = (pl.program_id(1) * bn) // cols_per_expert
        comb = comb_ref[...]
        lane = lax.broadcasted_iota(jnp.int32, comb.shape, 1)
        hid = hid * jnp.sum(jnp.where(lane == expert, comb, 0.0), axis=1, keepdims=True)
    o_ref[...] = hid.astype(o_ref.dtype)


def _swiglu(x, wg, wu, *, bm=1024, bn=512, comb=None, cols_per_expert=None):
    m, kdim = x.shape
    n = wg.shape[1]
    bm = min(bm, m)
    assert m % bm == 0 and n % bn == 0
    has_scale = comb is not None
    in_specs = [pl.BlockSpec((bm, kdim), lambda i, j: (i, 0)),
                pl.BlockSpec((kdim, bn), lambda i, j: (0, j)),
                pl.BlockSpec((kdim, bn), lambda i, j: (0, j))]
    args = [x, wg, wu]
    if has_scale:
        assert cols_per_expert % bn == 0
        in_specs.append(pl.BlockSpec((bm, V7X_LANES), lambda i, j: (i, 0)))
        args.append(comb)
    return pl.pallas_call(
        functools.partial(_swiglu_kernel, has_scale=has_scale, cols_per_expert=cols_per_expert, bn=bn),
        grid=(m // bm, n // bn),
        in_specs=in_specs,
        out_specs=pl.BlockSpec((bm, bn), lambda i, j: (i, j)),
        out_shape=jax.ShapeDtypeStruct((m, n), _BF16),
        compiler_params=pltpu.CompilerParams(
            dimension_semantics=("parallel", "parallel"),
            vmem_limit_bytes=_vmem_limit(2 * bm * kdim * 2, 4 * kdim * bn * 2, 2 * bm * bn * 2, 4 * bm * bn * 4)),
        name="swiglu",
    )(*args)


def _ada_kernel(c_ref, w_ref, b_ref, o_ref):
    c = c_ref[...]
    a = (c * jax.nn.sigmoid(c)).astype(_BF16)
    o_ref[...] = jnp.dot(a, w_ref[...].astype(_BF16), preferred_element_type=_F32) + b_ref[0]


def _ada(cond, ada_w, ada_b, layer, bn=512):
    rows = cond.shape[0]
    n = N_MOD * D_MODEL
    return pl.pallas_call(
        _ada_kernel,
        grid=(n // bn,),
        in_specs=[pl.BlockSpec((rows, D_MODEL), lambda j: (0, 0)),
                  pl.BlockSpec((None, D_MODEL, bn), lambda j: (layer, 0, j)),
                  pl.BlockSpec((None, 1, bn), lambda j: (layer, 0, j))],
        out_specs=pl.BlockSpec((rows, bn), lambda j: (0, j)),
        out_shape=jax.ShapeDtypeStruct((rows, n), _F32),
        compiler_params=pltpu.CompilerParams(
            dimension_semantics=("parallel",),
            vmem_limit_bytes=_vmem_limit(2 * D_MODEL * bn * 4, D_MODEL * bn * 2)),
        name="ada",
    )(cond, ada_w, ada_b.reshape(DEPTH, 1, n))


def _modnorm_kernel(*refs, with_router):
    if with_router:
        x_ref, w_ref, sh_ref, sc_ref, r_ref, h_ref, comb_ref = refs
    else:
        x_ref, w_ref, sh_ref, sc_ref, h_ref = refs
    x = x_ref[0]
    y = x * lax.rsqrt(jnp.mean(x * x, axis=-1, keepdims=True) + EPS)
    h = (y * w_ref[0]) * (1.0 + sc_ref[0]) + sh_ref[0]
    h_ref[...] = h.astype(h_ref.dtype)
    if with_router:
        logits = jnp.dot(h, r_ref[...], preferred_element_type=_F32, precision=lax.Precision.HIGHEST)
        lane = lax.broadcasted_iota(jnp.int32, logits.shape, 1)
        valid = lane < N_EXPERTS
        lg = jnp.where(valid, logits, -jnp.inf)
        v1 = jnp.max(lg, axis=-1, keepdims=True)
        i1 = jnp.min(jnp.where(lg == v1, lane, V7X_LANES), axis=-1, keepdims=True)
        lg2 = jnp.where(lane == i1, -jnp.inf, lg)
        v2 = jnp.max(lg2, axis=-1, keepdims=True)
        i2 = jnp.min(jnp.where(lg2 == v2, lane, V7X_LANES), axis=-1, keepdims=True)
        e2 = jnp.exp(v2 - v1)
        p1 = 1.0 / (1.0 + e2)
        p2 = e2 / (1.0 + e2)
        comb_ref[...] = jnp.where(lane == i1, p1, 0.0) + jnp.where(lane == i2, p2, 0.0)


def _modnorm(x, norm_w, shift, scale, router=None, bl=256):
    b, L, d = x.shape
    bl = min(bl, L)
    assert L % bl == 0
    nl = L // bl
    per_batch = shift.shape[0] == b and b > 1
    mod_map = (lambda i, j: (i, 0, 0)) if per_batch else (lambda i, j: (0, 0, 0))
    in_specs = [pl.BlockSpec((1, bl, d), lambda i, j: (i, j, 0)),
                pl.BlockSpec((1, d), lambda i, j: (0, 0)),
                pl.BlockSpec((1, 1, d), mod_map),
                pl.BlockSpec((1, 1, d), mod_map)]
    args = [x, norm_w.reshape(1, d), shift, scale]
    out_specs = [pl.BlockSpec((bl, d), lambda i, j: (i * nl + j, 0))]
    out_shape = [jax.ShapeDtypeStruct((b * L, d), _BF16)]
    with_router = router is not None
    if with_router:
        in_specs.append(pl.BlockSpec((d, V7X_LANES), lambda i, j: (0, 0)))
        args.append(router)
        out_specs.append(pl.BlockSpec((bl, V7X_LANES), lambda i, j: (i * nl + j, 0)))
        out_shape.append(jax.ShapeDtypeStruct((b * L, V7X_LANES), _F32))
    out = pl.pallas_call(
        functools.partial(_modnorm_kernel, with_router=with_router),
        grid=(b, nl),
        in_specs=in_specs,
        out_specs=out_specs,
        out_shape=out_shape,
        compiler_params=pltpu.CompilerParams(
            dimension_semantics=("parallel", "parallel"),
            vmem_limit_bytes=_vmem_limit(2 * bl * d * 4, 2 * bl * d * 2, 4 * bl * d * 4, 2 * d * V7X_LANES * 4)),
        name="modnorm",
    )(*args)
    return out if with_router else out[0]


def _final_norm_kernel(x_ref, w_ref, o_ref):
    x = x_ref[...]
    o_ref[...] = x * lax.rsqrt(jnp.mean(x * x, axis=-1, keepdims=True) + EPS) * w_ref[...]


def _final_norm(x, w, bl=256):
    m, d = x.shape
    return pl.pallas_call(
        _final_norm_kernel,
        grid=(m // bl,),
        in_specs=[pl.BlockSpec((bl, d), lambda i: (i, 0)), pl.BlockSpec((1, d), lambda i: (0, 0))],
        out_specs=pl.BlockSpec((bl, d), lambda i: (i, 0)),
        out_shape=jax.ShapeDtypeStruct((m, d), _F32),
        compiler_params=pltpu.CompilerParams(dimension_semantics=("parallel",),
                                             vmem_limit_bytes=_vmem_limit(6 * bl * d * 4)),
        name="final_norm",
    )(x, w.reshape(1, d))


def _heads(a, n_heads):
    b, L, W = a.shape
    return a.reshape(b, L, n_heads, W // n_heads).transpose(0, 2, 1, 3)


def _merge_heads(a):
    b, h, L, d = a.shape
    return a.transpose(0, 2, 1, 3).reshape(b, L, h * d)


def _head_rmsnorm(y, w, n_heads):
    b, L, W = y.shape
    yh = y.reshape(b, L, n_heads, W // n_heads)
    yh = yh * lax.rsqrt(jnp.mean(yh * yh, axis=-1, keepdims=True) + EPS)
    return yh.reshape(b, L, W) * w


def _short_conv(u, w, b, n_rows):
    bsz, L, C = u.shape
    seg = L // n_rows
    pad = SHORT_CONV // 2
    us = jnp.pad(u.reshape(bsz, n_rows, seg, C), ((0, 0), (0, 0), (pad, pad), (0, 0)))
    y = b
    for j in range(SHORT_CONV):
        y = y + w[j] * us[:, :, j:j + seg]
    return y.reshape(bsz, L, C)


def _to_chunks(a):
    b, h, L = a.shape[:3]
    a = a.reshape((b, h, L // CHUNK, CHUNK) + a.shape[3:])
    return jnp.moveaxis(a, 2, 0)


def _from_chunks(a):
    a = jnp.moveaxis(a, 0, 2)
    return a.reshape(a.shape[:2] + (-1,) + a.shape[4:])


def _rev(a):
    return None if a is None else jnp.flip(a, axis=2)


def _hgrn2_scan(q, k, v, g, s0):
    mask = jnp.tril(jnp.ones((CHUNK, CHUNK), dtype=bool))
    with_out = q is not None

    def step(s, inp):
        kc, vc, gc = inp[:3]
        b = jnp.cumsum(gc, axis=2)
        b_end = b[:, :, -1]
        s_new = jnp.exp(b_end)[..., None] * s + jnp.einsum(
            'bhck,bhcv->bhkv', kc * jnp.exp(b_end[:, :, None] - b), vc)
        if not with_out:
            return s_new, None
        qc = inp[3]
        m5 = mask[:, :, None]
        rel = jnp.where(m5, b[:, :, :, None] - b[:, :, None], 0.0)
        dec = jnp.where(m5, jnp.exp(rel), 0.0)
        att = jnp.einsum('bhtk,bhtsk,bhsk->bhts', qc, dec, kc)
        o = jnp.einsum('bhtk,bhkv->bhtv', qc * jnp.exp(b), s) + jnp.einsum('bhts,bhsv->bhtv', att, vc)
        return s_new, o

    xs = (_to_chunks(k), _to_chunks(v), _to_chunks(g)) + ((_to_chunks(q),) if with_out else ())
    s_fin, o = lax.scan(step, s0, xs)
    return s_fin, (_from_chunks(o) if with_out else None)


def _mlstm_scan(q, k, v, logi, logf, state0):
    mask = jnp.tril(jnp.ones((CHUNK, CHUNK), dtype=bool))
    with_out = q is not None

    def step(carry, inp):
        cm, nm, m = carry
        kc, vc, ic, fc = inp[:4]
        b = jnp.cumsum(fc, axis=-1)
        b_end = b[..., -1]
        log_w = b_end[..., None] - b + ic
        m_new = jnp.maximum(b_end + m, jnp.max(log_w, axis=-1))
        decay = jnp.exp(b_end + m - m_new)
        w = jnp.exp(log_w - m_new[..., None])
        c_new = decay[..., None, None] * cm + jnp.einsum('bhs,bhsk,bhsv->bhkv', w, kc, vc)
        n_new = decay[..., None] * nm + jnp.einsum('bhs,bhsk->bhk', w, kc)
        if not with_out:
            return (c_new, n_new, m_new), None
        qc = inp[4]
        log_d_raw = b[..., :, None] - b[..., None, :] + ic[..., None, :]
        log_prev = b + m[..., None]
        m_t = jnp.maximum(log_prev, jnp.max(jnp.where(mask, log_d_raw, NEG_BIG), axis=-1))
        prev = jnp.exp(log_prev - m_t)
        dmat = jnp.where(mask, jnp.exp(jnp.where(mask, log_d_raw - m_t[..., None], 0.0)), 0.0)
        s = jnp.einsum('bhtk,bhsk->bhts', qc, kc) * dmat
        num = prev[..., None] * jnp.einsum('bhtk,bhkv->bhtv', qc, cm) + jnp.einsum('bhts,bhsv->bhtv', s, vc)
        den = prev * jnp.einsum('bhtk,bhk->bht', qc, nm) + jnp.sum(s, axis=-1)
        h = num / jnp.maximum(jnp.abs(den), jnp.exp(-m_t))[..., None]
        return (c_new, n_new, m_new), h

    xs = (_to_chunks(k), _to_chunks(v), _to_chunks(logi), _to_chunks(logf)) + ((_to_chunks(q),) if with_out else ())
    fin, h = lax.scan(step, state0, xs)
    return fin, (_from_chunks(h) if with_out else None)


def _hyena_kernel_fft(L, w1, b1, w2, b2, w3, freq):
    pos = jnp.arange(L, dtype=_F32)
    t = pos / (L - 1)
    ang = (2.0 * math.pi / L) * pos
    bands = jnp.linspace(1e-4, HY_BANDS - 1, HY_BANDS, dtype=_F32)
    z = jnp.concatenate([t[:, None], jnp.cos(ang[:, None] * bands), jnp.sin(ang[:, None] * bands)], axis=-1)
    a = jnp.sin(freq[0] * (z @ w1 + b1))
    a = jnp.sin(freq[1] * (a @ w2 + b2))
    filt = (a @ w3).reshape(L, 2, HY_ORDER, HY_WIDTH)
    deltas = jnp.abs(jnp.linspace(HY_MIN_DECAY, HY_MAX_DECAY, HY_WIDTH, dtype=_F32))
    filt = filt * jnp.exp(-t[:, None] * deltas)[:, None, None, :]
    kern = jnp.concatenate([filt[:, 0], jnp.zeros((1, HY_ORDER, HY_WIDTH), _F32), filt[:L - 1, 1][::-1]], axis=0)
    kern = kern / (jnp.sum(jnp.abs(kern), axis=0, keepdims=True) + EPS)
    return jnp.fft.rfft(kern, axis=0)


def _hyena(p, lp, n_rows):
    L = p.shape[1]
    u = _short_conv(p, lp["hy_conv_w"], lp["hy_conv_b"], n_rows)
    v, x1, x2 = jnp.split(u, 3, axis=-1)
    kf = _hyena_kernel_fft(L, lp["hy_w1"], lp["hy_b1"], lp["hy_w2"], lp["hy_b2"], lp["hy_w3"], lp["hy_freq"])
    z = v
    for o, gate in enumerate((x1, x2)):
        conv = jnp.fft.irfft(jnp.fft.rfft(z, n=2 * L, axis=1) * kf[:, o], n=2 * L, axis=1)[:, :L]
        z = gate * (conv + lp["hy_skip"][o] * z)
    return z


def _zero_states(bsz):
    hg = jnp.zeros((bsz, HG_HEADS, HG_WIDTH // HG_HEADS, HG_WIDTH // HG_HEADS), _F32)
    ml = (jnp.zeros((bsz, ML_HEADS, ML_DH, ML_DH), _F32), jnp.zeros((bsz, ML_HEADS, ML_DH), _F32),
          jnp.zeros((bsz, ML_HEADS), _F32))
    return ((hg, hg), (ml, ml))


_O_HF, _O_HB, _O_HGI, _O_MLK, _O_MLV = 0, 1024, 2048, 3072, 5120
_O_HGQ, _O_HGOG, _O_MLQ, _O_MLOG, _O_HY, _O_MERGE = 7168, 8192, 9216, 11264, 13312, 16384


def _mixer_states_and_branches(p, gates, lp, lb, n_rows, init, with_out):
    def col(o, w):
        return p[..., o:o + w].astype(_F32)

    hf_pre, hb_pre, hg_i = col(_O_HF, HG_WIDTH), col(_O_HB, HG_WIDTH), col(_O_HGI, HG_WIDTH)
    ml_k, ml_v = col(_O_MLK, ML_WIDTH), col(_O_MLV, ML_WIDTH)
    lb = lb.astype(_F32)
    log_lb, log_1mlb = jnp.log(jnp.maximum(lb, LB_FLOOR)), jnp.log1p(-lb)
    hg_k, hg_g = [], []
    for d, pre in enumerate((hf_pre, hb_pre)):
        hg_g.append(_heads(jnp.logaddexp(log_lb[d], log_1mlb[d] + jax.nn.log_sigmoid(pre)), HG_HEADS))
        hg_k.append(_heads((1.0 - lb[d]) * jax.nn.sigmoid(-pre), HG_HEADS))
    hg_v = _heads(hg_i, HG_HEADS)

    conv_w, conv_b = lp["ml_conv_w"], lp["ml_conv_b"]
    ml_k = _heads(jax.nn.silu(_short_conv(ml_k, conv_w[:, :ML_WIDTH], conv_b[:ML_WIDTH], n_rows)) * ML_DH ** -0.5, ML_HEADS)
    ml_v = _heads(ml_v, ML_HEADS)
    gt = (gates + lp["ml_gate_b"]).reshape(gates.shape[:2] + (2, 2, ML_HEADS)).transpose(2, 3, 0, 4, 1)
    logi = (gt[0, 0], gt[1, 0])
    logf = (jax.nn.log_sigmoid(gt[0, 1]), jax.nn.log_sigmoid(gt[1, 1]))

    hg_q = ml_q = None
    if with_out:
        hg_q = _heads(jax.nn.silu(col(_O_HGQ, HG_WIDTH)), HG_HEADS)
        ml_q = _heads(jax.nn.silu(_short_conv(col(_O_MLQ, ML_WIDTH), conv_w[:, ML_WIDTH:], conv_b[ML_WIDTH:], n_rows)), ML_HEADS)

    hg_o, ml_o, fin_hg, fin_ml = [], [], [], []
    for d in range(2):
        tr = (lambda a: a) if d == 0 else _rev
        s_hg, o_hg = _hgrn2_scan(tr(hg_q), tr(hg_k[d]), tr(hg_v), tr(hg_g[d]), init[0][d])
        s_ml, o_ml = _mlstm_scan(tr(ml_q), tr(ml_k), tr(ml_v), tr(logi[d]), tr(logf[d]), init[1][d])
        fin_hg.append(s_hg)
        fin_ml.append(s_ml)
        if with_out:
            hg_o.append(tr(o_hg))
            ml_o.append(tr(o_ml))
    finals = ((fin_hg[0], fin_hg[1]), (fin_ml[0], fin_ml[1]))
    if not with_out:
        return None, finals
    y_hg = _head_rmsnorm(_merge_heads(hg_o[0] + hg_o[1]), lp["hg_norm_w"], HG_HEADS) * jax.nn.silu(col(_O_HGOG, HG_WIDTH))
    y_ml = _head_rmsnorm(_merge_heads(ml_o[0] + ml_o[1]), lp["ml_norm_w"], ML_HEADS) * jax.nn.sigmoid(col(_O_MLOG, ML_WIDTH))
    y_hy = _hyena(col(_O_HY, 3 * HY_WIDTH), lp, n_rows)
    return (y_hg, y_ml, y_hy), finals


def _prep_layer_weights(l, w_in, w_branch, w_out):
    wl = w_in[l]
    w_main = jnp.concatenate([wl[:, :N_STATE_MAIN], wl[:, N_STATE_COLS:]], axis=1).astype(_BF16)
    w_gate = jnp.pad(wl[:, N_STATE_MAIN:N_STATE_COLS], ((0, 0), (0, V7X_LANES - N_GATE_COLS))).astype(_BF16)
    return w_main, w_gate, w_branch[l].astype(_BF16), w_out[l].astype(_BF16)


def _stream_mixer(xs, mods, lw, lp, lb, n_rows, init, with_out):
    w_main, w_gate, wb, wo = lw
    bsz, L, d = xs.shape
    sh, sc, g = mods[0], mods[1], mods[2]
    h = _modnorm(xs, lp["norm1"], sh, sc)
    w_proj = w_main if with_out else w_main[:, :N_STATE_MAIN]
    p = _matmul(h, w_proj, out_dtype=_BF16).reshape(bsz, L, -1)
    gates = _matmul(h, w_gate, out_dtype=_F32).reshape(bsz, L, V7X_LANES)[..., :N_GATE_COLS]
    ys, finals = _mixer_states_and_branches(p, gates, lp, lb, n_rows, init, with_out)
    if not with_out:
        return None, finals
    y_hg, y_ml, y_hy = ys
    m = bsz * L
    merge_p = p[..., _O_MERGE:].astype(_F32).reshape(m, N_BRANCH * d)
    g_hg, g_ml, g_hy = jnp.split(jax.nn.sigmoid(merge_p), N_BRANCH, axis=-1)
    merged = (g_hg * _matmul(y_hg.reshape(m, -1).astype(_BF16), wb[:HG_WIDTH], out_dtype=_F32)
              + g_ml * _matmul(y_ml.reshape(m, -1).astype(_BF16), wb[HG_WIDTH:HG_WIDTH + ML_WIDTH], out_dtype=_F32)
              + g_hy * _matmul(y_hy.reshape(m, -1).astype(_BF16), wb[HG_WIDTH + ML_WIDTH:], out_dtype=_F32))
    x_new = _matmul(merged.astype(_BF16), wo, out_dtype=_F32, res=xs.reshape(m, d), gate=g,
                    rows_per_gate=L if g.shape[0] > 1 else m)
    return x_new.reshape(bsz, L, d), finals


def _stream_ffn(xs, mods, norm2, l, ffn):
    bsz, L, d = xs.shape
    m = bsz * L
    sh2, sc2, g2 = mods[3], mods[4], mods[5]
    rows_per_gate = L if g2.shape[0] > 1 else m
    if l % 2 == 0:
        wg, wu, wd = ffn["dense"]
        h = _modnorm(xs, norm2, sh2, sc2)
        hid = _swiglu(h, wg, wu)
        out = _matmul(hid, wd, out_dtype=_F32, bk=DENSE_FF_PAD // 4, res=xs.reshape(m, d), gate=g2,
                      rows_per_gate=rows_per_gate)
    else:
        router, wg, wu, wd = ffn["moe"]
        h, comb = _modnorm(xs, norm2, sh2, sc2, router=router)
        hid = _swiglu(h, wg, wu, comb=comb, cols_per_expert=EXPERT_FF)
        out = _matmul(hid, wd, out_dtype=_F32, bk=2048, res=xs.reshape(m, d), gate=g2,
                      rows_per_gate=rows_per_gate)
    return out.reshape(bsz, L, d)


def kernel(x, c, ctx, c_ctx, norm1_w, norm2_w, ada_w, ada_b, w_in, hg_lb_logits, hg_norm_w,
           ml_conv_w, ml_conv_b, ml_gate_b, ml_norm_w, hy_conv_w, hy_conv_b, hy_w1, hy_b1,
           hy_w2, hy_b2, hy_w3, hy_freq, hy_skip, w_branch, w_out, dense_w_gate, dense_w_up,
           dense_w_down, moe_router, moe_w_gate, moe_w_up, moe_w_down, final_norm_w):
    bsz, L, d = x.shape
    rows = L // GRID_W
    lb_w = jax.nn.softmax(hg_lb_logits.astype(_F32), axis=0)
    lower_bounds = jnp.cumsum(lb_w, axis=0) - lb_w[0]

    cond = jnp.zeros((8, d), _F32).at[:bsz].set(c).at[bsz].set(c_ctx)
    for l in range(DEPTH):
        lp = {
            "norm1": norm1_w[l], "hg_norm_w": hg_norm_w[l],
            "ml_conv_w": ml_conv_w[l], "ml_conv_b": ml_conv_b[l], "ml_gate_b": ml_gate_b[l],
            "ml_norm_w": ml_norm_w[l], "hy_conv_w": hy_conv_w[l], "hy_conv_b": hy_conv_b[l],
            "hy_w1": hy_w1[l], "hy_b1": hy_b1[l], "hy_w2": hy_w2[l], "hy_b2": hy_b2[l],
            "hy_w3": hy_w3[l], "hy_freq": hy_freq[l], "hy_skip": hy_skip[l],
        }
        lw = _prep_layer_weights(l, w_in, w_branch, w_out)
        if l % 2 == 0:
            i = l // 2
            pad = DENSE_FF_PAD - DENSE_FF
            ffn = {"dense": (jnp.pad(dense_w_gate[i].astype(_BF16), ((0, 0), (0, pad))),
                             jnp.pad(dense_w_up[i].astype(_BF16), ((0, 0), (0, pad))),
                             jnp.pad(dense_w_down[i].astype(_BF16), ((0, pad), (0, 0))))}
        else:
            i = l // 2
            ffn = {"moe": (jnp.pad(moe_router[i], ((0, 0), (0, V7X_LANES - N_EXPERTS))),
                           moe_w_gate[i].astype(_BF16).transpose(1, 0, 2).reshape(d, N_EXPERTS * EXPERT_FF),
                           moe_w_up[i].astype(_BF16).transpose(1, 0, 2).reshape(d, N_EXPERTS * EXPERT_FF),
                           moe_w_down[i].astype(_BF16).reshape(N_EXPERTS * EXPERT_FF, d))}
        mod = _ada(cond, ada_w, ada_b, l)
        mods_lat = [mod[:bsz, j * d:(j + 1) * d].reshape(bsz, 1, d) for j in range(N_MOD)]
        mods_ctx = [mod[bsz:bsz + 1, j * d:(j + 1) * d].reshape(1, 1, d) for j in range(N_MOD)]
        zero = _zero_states(bsz)
        last = l == DEPTH - 1
        ctx_new, ctx_fin = _stream_mixer(ctx, mods_ctx, lw, lp, lower_bounds[l], 1, zero, not last)
        if not last:
            ctx = _stream_ffn(ctx_new, mods_ctx, norm2_w[l], l, ffn)
        x, _ = _stream_mixer(x, mods_lat, lw, lp, lower_bounds[l], rows, ctx_fin, True)
        x = _stream_ffn(x, mods_lat, norm2_w[l], l, ffn)
    return _final_norm(x.reshape(bsz * L, d), final_norm_w).reshape(bsz, L, d)
```

```python
import functools
import math

import jax
import jax.numpy as jnp
from jax import lax
from jax.experimental import pallas as pl
from jax.experimental.pallas import tpu as pltpu

D_MODEL = 4096
DEPTH = 2
GRID_W = 64
EPS = 1e-6
SHORT_CONV = 3
N_MOD = 6
NEG_BIG = -1e30
LB_FLOOR = 1e-30

HG_WIDTH = D_MODEL // 4
HG_HEADS = 8
ML_WIDTH = D_MODEL // 2
ML_HEADS = 8
ML_DH = ML_WIDTH // ML_HEADS
HY_WIDTH = D_MODEL // 4
HY_ORDER = 2
HY_POS_DIM = 33
HY_BANDS = (HY_POS_DIM - 1) // 2
HY_MIN_DECAY = math.log(1e-2) / 1.5
HY_MAX_DECAY = math.log(1e-2) / 0.3
N_BRANCH = 3

N_GATE_COLS = 4 * ML_HEADS
N_STATE_MAIN = 3 * HG_WIDTH + 2 * ML_WIDTH
N_STATE_COLS = N_STATE_MAIN + N_GATE_COLS
N_MAIN = N_STATE_MAIN + 2 * HG_WIDTH + 2 * ML_WIDTH + 3 * HY_WIDTH + N_BRANCH * D_MODEL

DENSE_FF = 11008
N_EXPERTS = 8
TOP_K = 2
EXPERT_FF = 2048

V7X_LANES = 128
V7X_VMEM_BYTES = 64 * 1024 * 1024
DENSE_FF_PAD = 11264

_F32 = jnp.float32
_BF16 = jnp.bfloat16


def _vmem_limit(*block_bytes):
    need = int(sum(block_bytes)) + (4 << 20)
    return min(max(need, 16 << 20), V7X_VMEM_BYTES - (6 << 20))


def _mm_kernel(*refs, nk, has_res):
    if has_res:
        x_ref, w_ref, res_ref, gate_ref = refs[:4]
        rest = refs[4:]
    else:
        x_ref, w_ref = refs[:2]
        rest = refs[2:]
    o_ref = rest[0]

    def finish(acc):
        if has_res:
            acc = res_ref[...] + gate_ref[0] * acc
        o_ref[...] = acc.astype(o_ref.dtype)

    part = jnp.dot(x_ref[...], w_ref[...], preferred_element_type=_F32)
    if nk == 1:
        finish(part)
        return
    acc_ref = rest[1]
    k = pl.program_id(2)

    @pl.when(k == 0)
    def _():
        acc_ref[...] = part

    @pl.when(k > 0)
    def _():
        acc_ref[...] += part

    @pl.when(k == nk - 1)
    def _():
        finish(acc_ref[...])


def _matmul(x, w, *, out_dtype, bm=1024, bn=1024, bk=None, res=None, gate=None, rows_per_gate=None):
    m, kdim = x.shape
    n = w.shape[1]
    bm, bn = min(bm, m), min(bn, n)
    bk = kdim if bk is None else bk
    assert m % bm == 0 and n % bn == 0 and kdim % bk == 0, (x.shape, w.shape, bm, bn, bk)
    nk = kdim // bk
    has_res = res is not None
    in_specs = [pl.BlockSpec((bm, bk), lambda i, j, k: (i, k)),
                pl.BlockSpec((bk, bn), lambda i, j, k: (k, j))]
    args = [x, w]
    osz = jnp.dtype(out_dtype).itemsize
    vm = [2 * bm * bk * 2, 2 * bk * bn * 2, 2 * bm * bn * osz, 2 * bm * bn * 4]
    if has_res:
        blocks_per_gate = rows_per_gate // bm
        assert rows_per_gate % bm == 0
        in_specs += [pl.BlockSpec((bm, bn), lambda i, j, k: (i, j)),
                     pl.BlockSpec((1, 1, bn), lambda i, j, k: (i // blocks_per_gate, 0, j))]
        args += [res, gate]
        vm.append(2 * bm * bn * 4)
    scratch = [pltpu.VMEM((bm, bn), _F32)] if nk > 1 else []
    return pl.pallas_call(
        functools.partial(_mm_kernel, nk=nk, has_res=has_res),
        grid=(m // bm, n // bn, nk),
        in_specs=in_specs,
        out_specs=pl.BlockSpec((bm, bn), lambda i, j, k: (i, j)),
        out_shape=jax.ShapeDtypeStruct((m, n), out_dtype),
        scratch_shapes=scratch,
        compiler_params=pltpu.CompilerParams(
            dimension_semantics=("parallel", "parallel", "arbitrary"),
            vmem_limit_bytes=_vmem_limit(*vm)),
        name="matmul",
    )(*args)


def _swiglu_kernel(*refs, has_scale, cols_per_expert, bn):
    if has_scale:
        x_ref, wg_ref, wu_ref, comb_ref, o_ref = refs
    else:
        x_ref, wg_ref, wu_ref, o_ref = refs
    x = x_ref[...]
    g = jnp.dot(x, wg_ref[...], preferred_element_type=_F32)
    u = jnp.dot(x, wu_ref[...], preferred_element_type=_F32)
    hid = g * jax.nn.sigmoid(g) * u
    if has_scale:
        expert = (pl.program_id(1) * bn) // cols_per_expert
        comb = comb_ref[...]
        lane = lax.broadcasted_iota(jnp.int32, comb.shape, 1)
        hid = hid * jnp.sum(jnp.where(lane == expert, comb, 0.0), axis=1, keepdims=True)
    o_ref[...] = hid.astype(o_ref.dtype)


def _swiglu(x, wg, wu, *, bm=1024, bn=512, comb=None, cols_per_expert=None):
    m, kdim = x.shape
    n = wg.shape[1]
    bm = min(bm, m)
    assert m % bm == 0 and n % bn == 0
    has_scale = comb is not None
    in_specs = [pl.BlockSpec((bm, kdim), lambda i, j: (i, 0)),
                pl.BlockSpec((kdim, bn), lambda i, j: (0, j)),
                pl.BlockSpec((kdim, bn), lambda i, j: (0, j))]
    args = [x, wg, wu]
    if has_scale:
        assert cols_per_expert % bn == 0
        in_specs.append(pl.BlockSpec((bm, V7X_LANES), lambda i, j: (i, 0)))
        args.append(comb)
    return pl.pallas_call(
        functools.partial(_swiglu_kernel, has_scale=has_scale, cols_per_expert=cols_per_expert, bn=bn),
        grid=(m // bm, n // bn),
        in_specs=in_specs,
        out_specs=pl.BlockSpec((bm, bn), lambda i, j: (i, j)),
        out_shape=jax.ShapeDtypeStruct((m, n), _BF16),
        compiler_params=pltpu.CompilerParams(
            dimension_semantics=("parallel", "parallel"),
            vmem_limit_bytes=_vmem_limit(2 * bm * kdim * 2, 4 * kdim * bn * 2, 2 * bm * bn * 2, 4 * bm * bn * 4)),
        name="swiglu",
    )(*args)


def _ada_kernel(c_ref, w_ref, b_ref, o_ref):
    c = c_ref[...]
    a = (c * jax.nn.sigmoid(c)).astype(_BF16)
    o_ref[...] = jnp.dot(a, w_ref[...].astype(_BF16), preferred_element_type=_F32) + b_ref[0]


def _ada(cond, ada_w, ada_b, layer, bn=512):
    rows = cond.shape[0]
    n = N_MOD * D_MODEL
    return pl.pallas_call(
        _ada_kernel,
        grid=(n // bn,),
        in_specs=[pl.BlockSpec((rows, D_MODEL), lambda j: (0, 0)),
                  pl.BlockSpec((None, D_MODEL, bn), lambda j: (layer, 0, j)),
                  pl.BlockSpec((None, 1, bn), lambda j: (layer, 0, j))],
        out_specs=pl.BlockSpec((rows, bn), lambda j: (0, j)),
        out_shape=jax.ShapeDtypeStruct((rows, n), _F32),
        compiler_params=pltpu.CompilerParams(
            dimension_semantics=("parallel",),
            vmem_limit_bytes=_vmem_limit(2 * D_MODEL * bn * 4, D_MODEL * bn * 2)),
        name="ada",
    )(cond, ada_w, ada_b.reshape(DEPTH, 1, n))


def _modnorm_kernel(*refs, with_router):
    if with_router:
        x_ref, w_ref, sh_ref, sc_ref, r_ref, h_ref, comb_ref = refs
    else:
        x_ref, w_ref, sh_ref, sc_ref, h_ref = refs
    x = x_ref[0]
    y = x * lax.rsqrt(jnp.mean(x * x, axis=-1, keepdims=True) + EPS)
    h = (y * w_ref[0]) * (1.0 + sc_ref[0]) + sh_ref[0]
    h_ref[...] = h.astype(h_ref.dtype)
    if with_router:
        logits = jnp.dot(h, r_ref[...], preferred_element_type=_F32, precision=lax.Precision.HIGHEST)
        lane = lax.broadcasted_iota(jnp.int32, logits.shape, 1)
        valid = lane < N_EXPERTS
        lg = jnp.where(valid, logits, -jnp.inf)
        v1 = jnp.max(lg, axis=-1, keepdims=True)
        i1 = jnp.min(jnp.where(lg == v1, lane, V7X_LANES), axis=-1, keepdims=True)
        lg2 = jnp.where(lane == i1, -jnp.inf, lg)
        v2 = jnp.max(lg2, axis=-1, keepdims=True)
        i2 = jnp.min(jnp.where(lg2 == v2, lane, V7X_LANES), axis=-1, keepdims=True)
        e2 = jnp.exp(v2 - v1)
        p1 = 1.0 / (1.0 + e2)
        p2 = e2 / (1.0 + e2)
        comb_ref[...] = jnp.where(lane == i1, p1, 0.0) + jnp.where(lane == i2, p2, 0.0)


def _modnorm(x, norm_w, shift, scale, router=None, bl=256):
    b, L, d = x.shape
    bl = min(bl, L)
    assert L % bl == 0
    nl = L // bl
    per_batch = shift.shape[0] == b and b > 1
    mod_map = (lambda i, j: (i, 0, 0)) if per_batch else (lambda i, j: (0, 0, 0))
    in_specs = [pl.BlockSpec((1, bl, d), lambda i, j: (i, j, 0)),
                pl.BlockSpec((1, d), lambda i, j: (0, 0)),
                pl.BlockSpec((1, 1, d), mod_map),
                pl.BlockSpec((1, 1, d), mod_map)]
    args = [x, norm_w.reshape(1, d), shift, scale]
    out_specs = [pl.BlockSpec((bl, d), lambda i, j: (i * nl + j, 0))]
    out_shape = [jax.ShapeDtypeStruct((b * L, d), _BF16)]
    with_router = router is not None
    if with_router:
        in_specs.append(pl.BlockSpec((d, V7X_LANES), lambda i, j: (0, 0)))
        args.append(router)
        out_specs.append(pl.BlockSpec((bl, V7X_LANES), lambda i, j: (i * nl + j, 0)))
        out_shape.append(jax.ShapeDtypeStruct((b * L, V7X_LANES), _F32))
    out = pl.pallas_call(
        functools.partial(_modnorm_kernel, with_router=with_router),
        grid=(b, nl),
        in_specs=in_specs,
        out_specs=out_specs,
        out_shape=out_shape,
        compiler_params=pltpu.CompilerParams(
            dimension_semantics=("parallel", "parallel"),
            vmem_limit_bytes=_vmem_limit(2 * bl * d * 4, 2 * bl * d * 2, 4 * bl * d * 4, 2 * d * V7X_LANES * 4)),
        name="modnorm",
    )(*args)
    return out if with_router else out[0]


def _final_norm_kernel(x_ref, w_ref, o_ref):
    x = x_ref[...]
    o_ref[...] = x * lax.rsqrt(jnp.mean(x * x, axis=-1, keepdims=True) + EPS) * w_ref[...]


def _final_norm(x, w, bl=256):
    m, d = x.shape
    return pl.pallas_call(
        _final_norm_kernel,
        grid=(m // bl,),
        in_specs=[pl.BlockSpec((bl, d), lambda i: (i, 0)), pl.BlockSpec((1, d), lambda i: (0, 0))],
        out_specs=pl.BlockSpec((bl, d), lambda i: (i, 0)),
        out_shape=jax.ShapeDtypeStruct((m, d), _F32),
        compiler_params=pltpu.CompilerParams(dimension_semantics=("parallel",),
                                             vmem_limit_bytes=_vmem_limit(6 * bl * d * 4)),
        name="final_norm",
    )(x, w.reshape(1, d))


_O_HF, _O_HB, _O_HGI, _O_MLK, _O_MLV = 0, 1024, 2048, 3072, 5120
_O_HGQ, _O_HGOG, _O_MLQ, _O_MLOG, _O_HY, _O_MERGE = 7168, 8192, 9216, 11264, 13312, 16384

HG_DK = HG_WIDTH // HG_HEADS
HG_CHUNK = 64
HG_SUB = 16
HG_BLOCK = 512
ML_CHUNK = 256


def _log_sigmoid(x):
    return jnp.minimum(x, 0.0) - jnp.log1p(jnp.exp(-jnp.abs(x)))


def _silu(x):
    return x * jax.nn.sigmoid(x)


def _logaddexp(a, b):
    return jnp.maximum(a, b) + jnp.log1p(jnp.exp(-jnp.abs(a - b)))


def _hgrn2_kernel(*refs, rev, with_out, finalize, has_init, n_chunks):
    refs = list(refs)
    pre_ref, v_ref = refs.pop(0), refs.pop(0)
    q_ref = refs.pop(0) if with_out else None
    lbc_ref = refs.pop(0)
    s0_ref = refs.pop(0) if has_init else None
    if finalize:
        ofwd_ref, og_ref, nw_ref = refs.pop(0), refs.pop(0), refs.pop(0)
    o_ref = refs.pop(0) if with_out else None
    st_ref = refs.pop(0)
    b_sc, kk_sc = refs.pop(0), refs.pop(0)
    o_sc = refs.pop(0) if finalize else None
    C, SUB = HG_CHUNK, HG_SUB
    n_sub = C // SUB

    @pl.when(pl.program_id(2) == 0)
    def _():
        st_ref[0, 0] = s0_ref[0, 0] if has_init else jnp.zeros(st_ref.shape[2:], _F32)

    lbc = lbc_ref[...]
    log_lb, log_1mlb, one_m_lb = lbc[0:1], lbc[1:2], lbc[2:3]
    r_i = lax.broadcasted_iota(jnp.int32, (C, C), 0)
    c_i = lax.broadcasted_iota(jnp.int32, (C, C), 1)
    tri = ((c_i >= r_i) if rev else (c_i <= r_i)).astype(_F32)
    row64 = lax.broadcasted_iota(jnp.int32, (C, 1), 0)
    row16 = lax.broadcasted_iota(jnp.int32, (SUB, 1), 0)
    lane64 = lax.broadcasted_iota(jnp.int32, (SUB, C), 1)

    def chunk(ci, carry):
        c = (n_chunks - 1 - ci) if rev else ci
        r0 = pl.multiple_of(c * C, C)
        pre = pre_ref[0, pl.ds(r0, C), :].astype(_F32)
        v = v_ref[0, pl.ds(r0, C), :]
        g = _logaddexp(log_lb, log_1mlb + _log_sigmoid(pre))
        kk = one_m_lb * jax.nn.sigmoid(-pre)
        b = jnp.dot(tri, g, preferred_element_type=_F32, precision=lax.Precision.HIGHEST)
        last = 0 if rev else C - 1
        b_end = b[last:last + 1]
        st = st_ref[0, 0]
        kt = (kk * jnp.exp(b_end - b)).astype(_BF16)
        st_ref[0, 0] = st * jnp.exp(b_end) + lax.dot_general(
            v, kt, (((0,), (0,)), ((), ())), preferred_element_type=_F32)
        if not with_out:
            return carry
        q = _silu(q_ref[0, pl.ds(r0, C), :].astype(_F32))
        o = lax.dot_general((q * jnp.exp(b)).astype(_BF16), st.astype(_BF16),
                            (((1,), (1,)), ((), ())), preferred_element_type=_F32)
        bx = b - g
        b_sc[...] = b
        kk_sc[...] = kk
        atts = []
        for i in range(n_sub):
            first = i * SUB + (SUB - 1 if rev else 0)
            piv = bx[first:first + 1]
            qi = q[i * SUB:(i + 1) * SUB] * jnp.exp(b[i * SUB:(i + 1) * SUB] - piv)
            earlier = (row64 >= (i + 1) * SUB) if rev else (row64 < i * SUB)
            khat = jnp.where(earlier, kk * jnp.exp(jnp.where(earlier, piv - b, 0.0)), 0.0)
            atts.append(lax.dot_general(qi.astype(_BF16), khat.astype(_BF16),
                                        (((1,), (1,)), ((), ())), preferred_element_type=_F32))

        def diag(s, atts):
            out = []
            for i in range(n_sub):
                r = i * SUB + s
                brow = b_sc[pl.ds(r, 1), :]
                krow = kk_sc[pl.ds(r, 1), :]
                valid = (row16 <= s) if rev else (row16 >= s)
                rel = jnp.where(valid, b[i * SUB:(i + 1) * SUB] - brow, 0.0)
                z = jnp.where(valid, q[i * SUB:(i + 1) * SUB] * jnp.exp(rel) * krow, 0.0)
                col = jnp.sum(z, axis=1, keepdims=True)
                out.append(jnp.where(lane64 == r, col, atts[i]))
            return tuple(out)

        atts = lax.fori_loop(0, SUB, diag, tuple(atts))
        att = jnp.concatenate(atts, axis=0).astype(_BF16)
        o = o + jnp.dot(att, v, preferred_element_type=_F32)
        if finalize:
            o_sc[pl.ds(r0, C), :] = o
        else:
            o_ref[0, pl.ds(r0, C), :] = o
        return carry

    lax.fori_loop(0, n_chunks, chunk, 0)
    if finalize:
        y = ofwd_ref[0] + o_sc[...]
        y = y * lax.rsqrt(jnp.mean(y * y, axis=-1, keepdims=True) + EPS) * nw_ref[...]
        o_ref[0] = (y * _silu(og_ref[0].astype(_F32))).astype(o_ref.dtype)


def _hgrn2(p, lbc, *, d, with_out, init=None, o_fwd=None, norm_w=None):
    bsz, L, _ = p.shape
    rev = d == 1
    tb = min(L, HG_BLOCK)
    assert L % tb == 0 and tb % HG_CHUNK == 0
    nblk = L // tb
    finalize = o_fwd is not None
    has_init = init is not None
    blk = (lambda n: nblk - 1 - n) if rev else (lambda n: n)
    pre_off = (_O_HB if rev else _O_HF) // HG_DK

    def tok_spec(off):
        return pl.BlockSpec((1, tb, HG_DK), lambda b, h, n: (b, blk(n), off + h))

    in_specs = [tok_spec(pre_off), tok_spec(_O_HGI // HG_DK)]
    args = [p, p]
    if with_out:
        in_specs.append(tok_spec(_O_HGQ // HG_DK))
        args.append(p)
    in_specs.append(pl.BlockSpec((None, 8, HG_DK), lambda b, h, n: (d, 0, h)))
    args.append(lbc)
    state_spec = pl.BlockSpec((1, 1, HG_DK, HG_DK), lambda b, h, n: (b, h, 0, 0))
    if has_init:
        in_specs.append(state_spec)
        args.append(init)
    if finalize:
        in_specs += [tok_spec(0), tok_spec(_O_HGOG // HG_DK), pl.BlockSpec((1, HG_DK), lambda b, h, n: (0, h))]
        args += [o_fwd, p, norm_w.reshape(1, HG_WIDTH)]
    out_specs, out_shape = [], []
    if with_out:
        out_specs.append(tok_spec(0))
        out_shape.append(jax.ShapeDtypeStruct((bsz, L, HG_WIDTH), _BF16 if finalize else _F32))
    out_specs.append(state_spec)
    out_shape.append(jax.ShapeDtypeStruct((bsz, HG_HEADS, HG_DK, HG_DK), _F32))
    scratch = [pltpu.VMEM((HG_CHUNK, HG_DK), _F32), pltpu.VMEM((HG_CHUNK, HG_DK), _F32)]
    if finalize:
        scratch.append(pltpu.VMEM((tb, HG_DK), _F32))
    out = pl.pallas_call(
        functools.partial(_hgrn2_kernel, rev=rev, with_out=with_out, finalize=finalize, has_init=has_init,
                          n_chunks=tb // HG_CHUNK),
        grid=(bsz, HG_HEADS, nblk),
        in_specs=in_specs, out_specs=out_specs, out_shape=out_shape, scratch_shapes=scratch,
        compiler_params=pltpu.CompilerParams(dimension_semantics=("parallel", "parallel", "arbitrary")),
        name="hgrn2",
    )(*args)
    return (out[0], out[1]) if with_out else (None, out[0])


def _mlstm_kernel(*refs, rev, with_out, finalize, has_init, seg):
    refs = list(refs)
    k_ref, v_ref = refs.pop(0), refs.pop(0)
    q_ref = refs.pop(0) if with_out else None
    g_ref, ckw_ref, ckb_ref = refs.pop(0), refs.pop(0), refs.pop(0)
    if with_out:
        cqw_ref, cqb_ref = refs.pop(0), refs.pop(0)
    if has_init:
        c0_ref, n0_ref, m0_ref = refs.pop(0), refs.pop(0), refs.pop(0)
    if finalize:
        ofwd_ref, og_ref, nw_ref = refs.pop(0), refs.pop(0), refs.pop(0)
    o_ref = refs.pop(0) if with_out else None
    c_ref, n_ref, m_ref = refs
    C = k_ref.shape[1]

    @pl.when(pl.program_id(2) == 0)
    def _():
        if has_init:
            c_ref[...] = c0_ref[...]
            n_ref[...] = n0_ref[...]
            m_ref[...] = m0_ref[...]
        else:
            c_ref[...] = jnp.zeros(c_ref.shape, _F32)
            n_ref[...] = jnp.zeros(n_ref.shape, _F32)
            m_ref[...] = jnp.zeros(m_ref.shape, _F32)

    g = g_ref[0, 0]
    ji = 2 if rev else 0
    i_row = g[ji:ji + 1]
    f_row = _log_sigmoid(g[ji + 1:ji + 2])
    r_i = lax.broadcasted_iota(jnp.int32, (C, C), 0)
    c_i = lax.broadcasted_iota(jnp.int32, (C, C), 1)
    mask = (c_i >= r_i) if rev else (c_i <= r_i)
    upto = ((r_i >= c_i) if rev else (r_i <= c_i)).astype(_F32)
    b_row = jnp.dot(jnp.broadcast_to(f_row, (8, C)), upto, preferred_element_type=_F32,
                    precision=lax.Precision.HIGHEST)[0:1]
    b_col = jnp.sum(jnp.where(mask, f_row, 0.0), axis=1, keepdims=True)
    i_col = jnp.sum(jnp.where(r_i == c_i, i_row, 0.0), axis=1, keepdims=True)
    b_end = jnp.sum(f_row, axis=1, keepdims=True)
    m_prev = m_ref[0, 0][0:1, 0:1]
    cm = c_ref[0, 0]
    nrow = n_ref[0, 0]

    pos = lax.broadcasted_iota(jnp.int32, (C, 1), 0) % seg

    def conv(x_ref, w_ref, b_ref):
        x = x_ref[0].astype(_F32)
        w = w_ref[...]
        x_prev = jnp.where(pos == 0, 0.0, pltpu.roll(x, 1, 0))
        x_next = jnp.where(pos == seg - 1, 0.0, pltpu.roll(x, C - 1, 0))
        return b_ref[...] + w[0:1] * x_prev + w[1:2] * x + w[2:3] * x_next

    k = _silu(conv(k_ref, ckw_ref, ckb_ref)) * ML_DH ** -0.5
    v = v_ref[0]

    if with_out:
        q = _silu(conv(q_ref, cqw_ref, cqb_ref))
        qb = q.astype(_BF16)
        logd = b_col - b_row + i_row
        m_t = jnp.maximum(b_col + m_prev, jnp.max(jnp.where(mask, logd, NEG_BIG), axis=1, keepdims=True))
        prev = jnp.exp(b_col + m_prev - m_t)
        dmat = jnp.where(mask, jnp.exp(jnp.where(mask, logd - m_t, 0.0)), 0.0)
        s = lax.dot_general(qb, k.astype(_BF16), (((1,), (1,)), ((), ())), preferred_element_type=_F32) * dmat
        num = prev * jnp.dot(qb, cm.astype(_BF16), preferred_element_type=_F32) \
            + jnp.dot(s.astype(_BF16), v, preferred_element_type=_F32)
        den = prev * jnp.sum(q * nrow, axis=1, keepdims=True) + jnp.sum(s, axis=1, keepdims=True)
        h = num / jnp.maximum(jnp.abs(den), jnp.exp(-m_t))
        if finalize:
            y = ofwd_ref[0] + h
            y = y * lax.rsqrt(jnp.mean(y * y, axis=-1, keepdims=True) + EPS) * nw_ref[...]
            o_ref[0] = (y * jax.nn.sigmoid(og_ref[0].astype(_F32))).astype(o_ref.dtype)
        else:
            o_ref[0] = h

    log_w = b_end - b_col + i_col
    m_new = jnp.maximum(b_end + m_prev, jnp.max(log_w, axis=0, keepdims=True))
    decay = jnp.exp(b_end + m_prev - m_new)
    wk = jnp.exp(log_w - m_new) * k
    c_ref[0, 0] = decay * cm + lax.dot_general(wk.astype(_BF16), v, (((0,), (0,)), ((), ())),
                                               preferred_element_type=_F32)
    n_ref[0, 0] = decay * nrow + jnp.sum(wk, axis=0, keepdims=True)
    m_ref[0, 0] = jnp.broadcast_to(m_new, m_ref.shape[2:])


def _mlstm(p, gate_rows, conv_w, conv_b, layer, *, d, seg, with_out, init=None, o_fwd=None, norm_w=None):
    bsz, L, _ = p.shape
    rev = d == 1
    C = min(L, ML_CHUNK)
    assert L % C == 0 and C % seg == 0
    nch = L // C
    finalize = o_fwd is not None
    has_init = init is not None
    blk = (lambda n: nch - 1 - n) if rev else (lambda n: n)

    def tok_spec(off):
        return pl.BlockSpec((1, C, ML_DH), lambda b, h, n: (b, blk(n), off + h))

    def conv_specs(off):
        return [pl.BlockSpec((None, SHORT_CONV, ML_DH), lambda b, h, n: (layer, 0, off + h)),
                pl.BlockSpec((None, 1, ML_DH), lambda b, h, n: (layer, 0, off + h))]

    conv_b3 = conv_b.reshape(DEPTH, 1, 2 * ML_WIDTH)
    in_specs = [tok_spec(_O_MLK // ML_DH), tok_spec(_O_MLV // ML_DH)]
    args = [p, p]
    if with_out:
        in_specs.append(tok_spec(_O_MLQ // ML_DH))
        args.append(p)
    in_specs.append(pl.BlockSpec((1, 1, 8, C), lambda b, h, n: (b, h, 0, blk(n))))
    args.append(gate_rows)
    in_specs += conv_specs(0)
    args += [conv_w, conv_b3]
    if with_out:
        in_specs += conv_specs(ML_HEADS)
        args += [conv_w, conv_b3]
    state_specs = [pl.BlockSpec((1, 1, ML_DH, ML_DH), lambda b, h, n: (b, h, 0, 0)),
                   pl.BlockSpec((1, 1, 1, ML_DH), lambda b, h, n: (b, h, 0, 0)),
                   pl.BlockSpec((1, 1, 8, V7X_LANES), lambda b, h, n: (b, h, 0, 0))]
    state_shape = [jax.ShapeDtypeStruct((bsz, ML_HEADS, ML_DH, ML_DH), _F32),
                   jax.ShapeDtypeStruct((bsz, ML_HEADS, 1, ML_DH), _F32),
                   jax.ShapeDtypeStruct((bsz, ML_HEADS, 8, V7X_LANES), _F32)]
    if has_init:
        in_specs += state_specs
        args += list(init)
    if finalize:
        in_specs += [tok_spec(0), tok_spec(_O_MLOG // ML_DH), pl.BlockSpec((1, ML_DH), lambda b, h, n: (0, h))]
        args += [o_fwd, p, norm_w.reshape(1, ML_WIDTH)]
    out_specs, out_shape = [], []
    if with_out:
        out_specs.append(tok_spec(0))
        out_shape.append(jax.ShapeDtypeStruct((bsz, L, ML_WIDTH), _BF16 if finalize else _F32))
    out = pl.pallas_call(
        functools.partial(_mlstm_kernel, rev=rev, with_out=with_out, finalize=finalize, has_init=has_init, seg=seg),
        grid=(bsz, ML_HEADS, nch),
        in_specs=in_specs, out_specs=out_specs + state_specs, out_shape=out_shape + state_shape,
        compiler_params=pltpu.CompilerParams(dimension_semantics=("parallel", "parallel", "arbitrary")),
        name="mlstm",
    )(*args)
    return (out[0], tuple(out[1:])) if with_out else (None, tuple(out))


def _short_conv(u, w, b, n_rows):
    bsz, L, C = u.shape
    seg = L // n_rows
    pad = SHORT_CONV // 2
    us = jnp.pad(u.reshape(bsz, n_rows, seg, C), ((0, 0), (0, 0), (pad, pad), (0, 0)))
    y = b
    for j in range(SHORT_CONV):
        y = y + w[j] * us[:, :, j:j + seg]
    return y.reshape(bsz, L, C)


def _hyena_kernel_fft(L, w1, b1, w2, b2, w3, freq):
    pos = jnp.arange(L, dtype=_F32)
    t = pos / (L - 1)
    ang = (2.0 * math.pi / L) * pos
    bands = jnp.linspace(1e-4, HY_BANDS - 1, HY_BANDS, dtype=_F32)
    z = jnp.concatenate([t[:, None], jnp.cos(ang[:, None] * bands), jnp.sin(ang[:, None] * bands)], axis=-1)
    a = jnp.sin(freq[0] * (z @ w1 + b1))
    a = jnp.sin(freq[1] * (a @ w2 + b2))
    filt = (a @ w3).reshape(L, 2, HY_ORDER, HY_WIDTH)
    deltas = jnp.abs(jnp.linspace(HY_MIN_DECAY, HY_MAX_DECAY, HY_WIDTH, dtype=_F32))
    filt = filt * jnp.exp(-t[:, None] * deltas)[:, None, None, :]
    kern = jnp.concatenate([filt[:, 0], jnp.zeros((1, HY_ORDER, HY_WIDTH), _F32), filt[:L - 1, 1][::-1]], axis=0)
    kern = kern / (jnp.sum(jnp.abs(kern), axis=0, keepdims=True) + EPS)
    return jnp.fft.rfft(kern, axis=0)


def _hyena(p, lp, n_rows):
    L = p.shape[1]
    u = _short_conv(p, lp["hy_conv_w"], lp["hy_conv_b"], n_rows)
    v, x1, x2 = jnp.split(u, 3, axis=-1)
    kf = _hyena_kernel_fft(L, lp["hy_w1"], lp["hy_b1"], lp["hy_w2"], lp["hy_b2"], lp["hy_w3"], lp["hy_freq"])
    z = v
    for o, gate in enumerate((x1, x2)):
        conv = jnp.fft.irfft(jnp.fft.rfft(z, n=2 * L, axis=1) * kf[:, o], n=2 * L, axis=1)[:, :L]
        z = gate * (conv + lp["hy_skip"][o] * z)
    return z


def _prep_layer_weights(l, w_in, w_branch, w_out):
    wl = w_in[l]
    w_main = jnp.concatenate([wl[:, :N_STATE_MAIN], wl[:, N_STATE_COLS:]], axis=1).astype(_BF16)
    w_gate = jnp.pad(wl[:, N_STATE_MAIN:N_STATE_COLS], ((0, 0), (0, V7X_LANES - N_GATE_COLS))).astype(_BF16)
    return w_main, w_gate, w_branch[l].astype(_BF16), w_out[l].astype(_BF16)


def _lower_bound_rows(lb):
    lb = lb.astype(_F32)
    rows = jnp.stack([jnp.log(jnp.maximum(lb, LB_FLOOR)), jnp.log1p(-lb), 1.0 - lb], axis=1)
    return jnp.pad(rows, ((0, 0), (0, 5), (0, 0)))


def _gate_rows(gates, gate_b):
    bsz, L, _ = gates.shape
    g = (gates[..., :N_GATE_COLS] + gate_b).reshape(bsz, L, 4, ML_HEADS)
    return jnp.pad(g.transpose(0, 3, 2, 1), ((0, 0), (0, 0), (0, 4), (0, 0)))


def _stream_mixer(xs, mods, lw, lp, layer, n_rows, init, with_out):
    w_main, w_gate, wb, wo = lw
    bsz, L, d = xs.shape
    sh, sc, g = mods[0], mods[1], mods[2]
    h = _modnorm(xs, lp["norm1"], sh, sc)
    w_proj = w_main if with_out else w_main[:, :N_STATE_MAIN]
    p = _matmul(h, w_proj, out_dtype=_BF16).reshape(bsz, L, -1)
    gates = _matmul(h, w_gate, out_dtype=_F32).reshape(bsz, L, V7X_LANES)
    grows = _gate_rows(gates, lp["ml_gate_b"])
    seg = L // n_rows
    hg_init, ml_init = init if init is not None else ((None, None), (None, None))
    hg_kw = dict(with_out=with_out)
    ml_kw = dict(seg=seg, with_out=with_out)
    o_hg, s_hg_f = _hgrn2(p, lp["lbc"], d=0, init=hg_init[0], **hg_kw)
    y_hg, s_hg_b = _hgrn2(p, lp["lbc"], d=1, init=hg_init[1], o_fwd=o_hg, norm_w=lp["hg_norm_w"], **hg_kw)
    o_ml, s_ml_f = _mlstm(p, grows, lp["ml_conv_w"], lp["ml_conv_b"], layer, d=0, init=ml_init[0], **ml_kw)
    y_ml, s_ml_b = _mlstm(p, grows, lp["ml_conv_w"], lp["ml_conv_b"], layer, d=1, init=ml_init[1],
                          o_fwd=o_ml, norm_w=lp["ml_norm_w"], **ml_kw)
    finals = ((s_hg_f, s_hg_b), (s_ml_f, s_ml_b))
    if not with_out:
        return None, finals
    y_hy = _hyena(p[..., _O_HY:_O_HY + 3 * HY_WIDTH].astype(_F32), lp, n_rows)
    m = bsz * L
    merge_p = p[..., _O_MERGE:].astype(_F32).reshape(m, N_BRANCH * d)
    g_hg, g_ml, g_hy = jnp.split(jax.nn.sigmoid(merge_p), N_BRANCH, axis=-1)
    merged = (g_hg * _matmul(y_hg.reshape(m, -1), wb[:HG_WIDTH], out_dtype=_F32)
              + g_ml * _matmul(y_ml.reshape(m, -1), wb[HG_WIDTH:HG_WIDTH + ML_WIDTH], out_dtype=_F32)
              + g_hy * _matmul(y_hy.reshape(m, -1).astype(_BF16), wb[HG_WIDTH + ML_WIDTH:], out_dtype=_F32))
    x_new = _matmul(merged.astype(_BF16), wo, out_dtype=_F32, res=xs.reshape(m, d), gate=g,
                    rows_per_gate=L if g.shape[0] > 1 else m)
    return x_new.reshape(bsz, L, d), finals


def _stream_ffn(xs, mods, norm2, l, ffn):
    bsz, L, d = xs.shape
    m = bsz * L
    sh2, sc2, g2 = mods[3], mods[4], mods[5]
    rows_per_gate = L if g2.shape[0] > 1 else m
    if l % 2 == 0:
        wg, wu, wd = ffn["dense"]
        h = _modnorm(xs, norm2, sh2, sc2)
        hid = _swiglu(h, wg, wu)
        out = _matmul(hid, wd, out_dtype=_F32, bk=DENSE_FF_PAD // 4, res=xs.reshape(m, d), gate=g2,
                      rows_per_gate=rows_per_gate)
    else:
        router, wg, wu, wd = ffn["moe"]
        h, comb = _modnorm(xs, norm2, sh2, sc2, router=router)
        hid = _swiglu(h, wg, wu, comb=comb, cols_per_expert=EXPERT_FF)
        out = _matmul(hid, wd, out_dtype=_F32, bk=2048, res=xs.reshape(m, d), gate=g2,
                      rows_per_gate=rows_per_gate)
    return out.reshape(bsz, L, d)


def kernel(x, c, ctx, c_ctx, norm1_w, norm2_w, ada_w, ada_b, w_in, hg_lb_logits, hg_norm_w,
           ml_conv_w, ml_conv_b, ml_gate_b, ml_norm_w, hy_conv_w, hy_conv_b, hy_w1, hy_b1,
           hy_w2, hy_b2, hy_w3, hy_freq, hy_skip, w_branch, w_out, dense_w_gate, dense_w_up,
           dense_w_down, moe_router, moe_w_gate, moe_w_up, moe_w_down, final_norm_w):
    bsz, L, d = x.shape
    rows = L // GRID_W
    lb_w = jax.nn.softmax(hg_lb_logits.astype(_F32), axis=0)
    lower_bounds = jnp.cumsum(lb_w, axis=0) - lb_w[0]

    cond = jnp.zeros((8, d), _F32).at[:bsz].set(c).at[bsz].set(c_ctx)
    for l in range(DEPTH):
        lp = {
            "norm1": norm1_w[l], "hg_norm_w": hg_norm_w[l], "lbc": _lower_bound_rows(lower_bounds[l]),
            "ml_conv_w": ml_conv_w, "ml_conv_b": ml_conv_b, "ml_gate_b": ml_gate_b[l],
            "ml_norm_w": ml_norm_w[l], "hy_conv_w": hy_conv_w[l], "hy_conv_b": hy_conv_b[l],
            "hy_w1": hy_w1[l], "hy_b1": hy_b1[l], "hy_w2": hy_w2[l], "hy_b2": hy_b2[l],
            "hy_w3": hy_w3[l], "hy_freq": hy_freq[l], "hy_skip": hy_skip[l],
        }
        lw = _prep_layer_weights(l, w_in, w_branch, w_out)
        if l % 2 == 0:
            i = l // 2
            pad = DENSE_FF_PAD - DENSE_FF
            ffn = {"dense": (jnp.pad(dense_w_gate[i].astype(_BF16), ((0, 0), (0, pad))),
                             jnp.pad(dense_w_up[i].astype(_BF16), ((0, 0), (0, pad))),
                             jnp.pad(dense_w_down[i].astype(_BF16), ((0, pad), (0, 0))))}
        else:
            i = l // 2
            ffn = {"moe": (jnp.pad(moe_router[i], ((0, 0), (0, V7X_LANES - N_EXPERTS))),
                           moe_w_gate[i].astype(_BF16).transpose(1, 0, 2).reshape(d, N_EXPERTS * EXPERT_FF),
                           moe_w_up[i].astype(_BF16).transpose(1, 0, 2).reshape(d, N_EXPERTS * EXPERT_FF),
                           moe_w_down[i].astype(_BF16).reshape(N_EXPERTS * EXPERT_FF, d))}
        mod = _ada(cond, ada_w, ada_b, l)
        mods_lat = [mod[:bsz, j * d:(j + 1) * d].reshape(bsz, 1, d) for j in range(N_MOD)]
        mods_ctx = [mod[bsz:bsz + 1, j * d:(j + 1) * d].reshape(1, 1, d) for j in range(N_MOD)]
        last = l == DEPTH - 1
        ctx_new, ctx_fin = _stream_mixer(ctx, mods_ctx, lw, lp, l, 1, None, not last)
        if not last:
            ctx = _stream_ffn(ctx_new, mods_ctx, norm2_w[l], l, ffn)
        x, _ = _stream_mixer(x, mods_lat, lw, lp, l, rows, ctx_fin, True)
        x = _stream_ffn(x, mods_lat, norm2_w[l], l, ffn)
    return _final_norm(x.reshape(bsz * L, d), final_norm_w).reshape(bsz, L, d)
```

```python
import functools
import math

import jax
import jax.numpy as jnp
from jax import lax
from jax.experimental import pallas as pl
from jax.experimental.pallas import tpu as pltpu

D_MODEL = 4096
DEPTH = 2
GRID_W = 64
EPS = 1e-6
SHORT_CONV = 3
N_MOD = 6
NEG_BIG = -1e30
LB_FLOOR = 1e-30

HG_WIDTH = D_MODEL // 4
HG_HEADS = 8
ML_WIDTH = D_MODEL // 2
ML_HEADS = 8
ML_DH = ML_WIDTH // ML_HEADS
HY_WIDTH = D_MODEL // 4
HY_ORDER = 2
HY_POS_DIM = 33
HY_BANDS = (HY_POS_DIM - 1) // 2
HY_MIN_DECAY = math.log(1e-2) / 1.5
HY_MAX_DECAY = math.log(1e-2) / 0.3
N_BRANCH = 3

N_GATE_COLS = 4 * ML_HEADS
N_STATE_MAIN = 3 * HG_WIDTH + 2 * ML_WIDTH
N_STATE_COLS = N_STATE_MAIN + N_GATE_COLS
N_MAIN = N_STATE_MAIN + 2 * HG_WIDTH + 2 * ML_WIDTH + 3 * HY_WIDTH + N_BRANCH * D_MODEL

DENSE_FF = 11008
N_EXPERTS = 8
TOP_K = 2
EXPERT_FF = 2048

V7X_LANES = 128
V7X_VMEM_BYTES = 64 * 1024 * 1024
DENSE_FF_PAD = 11264

_F32 = jnp.float32
_BF16 = jnp.bfloat16


def _vmem_limit(*block_bytes):
    need = int(sum(block_bytes)) + (4 << 20)
    return min(max(need, 16 << 20), V7X_VMEM_BYTES - (6 << 20))


def _mm_kernel(*refs, nk, has_res):
    if has_res:
        x_ref, w_ref, res_ref, gate_ref = refs[:4]
        rest = refs[4:]
    else:
        x_ref, w_ref = refs[:2]
        rest = refs[2:]
    o_ref = rest[0]

    def finish(acc):
        if has_res:
            acc = res_ref[...] + gate_ref[0] * acc
        o_ref[...] = acc.astype(o_ref.dtype)

    part = jnp.dot(x_ref[...], w_ref[...], preferred_element_type=_F32)
    if nk == 1:
        finish(part)
        return
    acc_ref = rest[1]
    k = pl.program_id(2)

    @pl.when(k == 0)
    def _():
        acc_ref[...] = part

    @pl.when(k > 0)
    def _():
        acc_ref[...] += part

    @pl.when(k == nk - 1)
    def _():
        finish(acc_ref[...])


def _matmul(x, w, *, out_dtype, bm=1024, bn=1024, bk=None, res=None, gate=None, rows_per_gate=None):
    m, kdim = x.shape
    n = w.shape[1]
    bm, bn = min(bm, m), min(bn, n)
    bk = kdim if bk is None else bk
    assert m % bm == 0 and n % bn == 0 and kdim % bk == 0, (x.shape, w.shape, bm, bn, bk)
    nk = kdim // bk
    has_res = res is not None
    in_specs = [pl.BlockSpec((bm, bk), lambda i, j, k: (i, k)),
                pl.BlockSpec((bk, bn), lambda i, j, k: (k, j))]
    args = [x, w]
    osz = jnp.dtype(out_dtype).itemsize
    vm = [2 * bm * bk * 2, 2 * bk * bn * 2, 2 * bm * bn * osz, 2 * bm * bn * 4]
    if has_res:
        blocks_per_gate = rows_per_gate // bm
        assert rows_per_gate % bm == 0
        in_specs += [pl.BlockSpec((bm, bn), lambda i, j, k: (i, j)),
                     pl.BlockSpec((1, 1, bn), lambda i, j, k: (i // blocks_per_gate, 0, j))]
        args += [res, gate]
        vm.append(2 * bm * bn * 4)
    scratch = [pltpu.VMEM((bm, bn), _F32)] if nk > 1 else []
    return pl.pallas_call(
        functools.partial(_mm_kernel, nk=nk, has_res=has_res),
        grid=(m // bm, n // bn, nk),
        in_specs=in_specs,
        out_specs=pl.BlockSpec((bm, bn), lambda i, j, k: (i, j)),
        out_shape=jax.ShapeDtypeStruct((m, n), out_dtype),
        scratch_shapes=scratch,
        compiler_params=pltpu.CompilerParams(
            dimension_semantics=("parallel", "parallel", "arbitrary"),
            vmem_limit_bytes=_vmem_limit(*vm)),
        name="matmul",
    )(*args)


def _swiglu_kernel(*refs, has_scale, cols_per_expert, bn):
    if has_scale:
        x_ref, wg_ref, wu_ref, comb_ref, o_ref = refs
    else:
        x_ref, wg_ref, wu_ref, o_ref = refs
    x = x_ref[...]
    g = jnp.dot(x, wg_ref[...], preferred_element_type=_F32)
    u = jnp.dot(x, wu_ref[...], preferred_element_type=_F32)
    hid = g * jax.nn.sigmoid(g) * u
    if has_scale:
        expert = (pl.program_id(1) * bn) // cols_per_expert
        comb = comb_ref[...]
        lane = lax.broadcasted_iota(jnp.int32, comb.shape, 1)
        hid = hid * jnp.sum(jnp.where(lane == expert, comb, 0.0), axis=1, keepdims=True)
    o_ref[...] = hid.astype(o_ref.dtype)


def _swiglu(x, wg, wu, *, bm=1024, bn=512, comb=None, cols_per_expert=None):
    m, kdim = x.shape
    n = wg.shape[1]
    bm = min(bm, m)
    assert m % bm == 0 and n % bn == 0
    has_scale = comb is not None
    in_specs = [pl.BlockSpec((bm, kdim), lambda i, j: (i, 0)),
                pl.BlockSpec((kdim, bn), lambda i, j: (0, j)),
                pl.BlockSpec((kdim, bn), lambda i, j: (0, j))]
    args = [x, wg, wu]
    if has_scale:
        assert cols_per_expert % bn == 0
        in_specs.append(pl.BlockSpec((bm, V7X_LANES), lambda i, j: (i, 0)))
        args.append(comb)
    return pl.pallas_call(
        functools.partial(_swiglu_kernel, has_scale=has_scale, cols_per_expert=cols_per_expert, bn=bn),
        grid=(m // bm, n // bn),
        in_specs=in_specs,
        out_specs=pl.BlockSpec((bm, bn), lambda i, j: (i, j)),
        out_shape=jax.ShapeDtypeStruct((m, n), _BF16),
        compiler_params=pltpu.CompilerParams(
            dimension_semantics=("parallel", "parallel"),
            vmem_limit_bytes=_vmem_limit(2 * bm * kdim * 2, 4 * kdim * bn * 2, 2 * bm * bn * 2, 4 * bm * bn * 4)),
        name="swiglu",
    )(*args)


def _ada_kernel(c_ref, w_ref, b_ref, o_ref):
    c = c_ref[...]
    a = (c * jax.nn.sigmoid(c)).astype(_BF16)
    o_ref[...] = jnp.dot(a, w_ref[...].astype(_BF16), preferred_element_type=_F32) + b_ref[0]


def _ada(cond, ada_w, ada_b, layer, bn=512):
    rows = cond.shape[0]
    n = N_MOD * D_MODEL
    return pl.pallas_call(
        _ada_kernel,
        grid=(n // bn,),
        in_specs=[pl.BlockSpec((rows, D_MODEL), lambda j: (0, 0)),
                  pl.BlockSpec((None, D_MODEL, bn), lambda j: (layer, 0, j)),
                  pl.BlockSpec((None, 1, bn), lambda j: (layer, 0, j))],
        out_specs=pl.BlockSpec((rows, bn), lambda j: (0, j)),
        out_shape=jax.ShapeDtypeStruct((rows, n), _F32),
        compiler_params=pltpu.CompilerParams(
            dimension_semantics=("parallel",),
            vmem_limit_bytes=_vmem_limit(2 * D_MODEL * bn * 4, D_MODEL * bn * 2)),
        name="ada",
    )(cond, ada_w, ada_b.reshape(DEPTH, 1, n))


def _modnorm_kernel(*refs, with_router):
    if with_router:
        x_ref, w_ref, sh_ref, sc_ref, r_ref, h_ref, comb_ref = refs
    else:
        x_ref, w_ref, sh_ref, sc_ref, h_ref = refs
    x = x_ref[0]
    y = x * lax.rsqrt(jnp.mean(x * x, axis=-1, keepdims=True) + EPS)
    h = (y * w_ref[0]) * (1.0 + sc_ref[0]) + sh_ref[0]
    h_ref[...] = h.astype(h_ref.dtype)
    if with_router:
        logits = jnp.dot(h, r_ref[...], preferred_element_type=_F32, precision=lax.Precision.HIGHEST)
        lane = lax.broadcasted_iota(jnp.int32, logits.shape, 1)
        valid = lane < N_EXPERTS
        lg = jnp.where(valid, logits, -jnp.inf)
        v1 = jnp.max(lg, axis=-1, keepdims=True)
        i1 = jnp.min(jnp.where(lg == v1, lane, V7X_LANES), axis=-1, keepdims=True)
        lg2 = jnp.where(lane == i1, -jnp.inf, lg)
        v2 = jnp.max(lg2, axis=-1, keepdims=True)
        i2 = jnp.min(jnp.where(lg2 == v2, lane, V7X_LANES), axis=-1, keepdims=True)
        e2 = jnp.exp(v2 - v1)
        p1 = 1.0 / (1.0 + e2)
        p2 = e2 / (1.0 + e2)
        comb_ref[...] = jnp.where(lane == i1, p1, 0.0) + jnp.where(lane == i2, p2, 0.0)


def _modnorm(x, norm_w, shift, scale, router=None, bl=256):
    b, L, d = x.shape
    bl = min(bl, L)
    assert L % bl == 0
    nl = L // bl
    per_batch = shift.shape[0] == b and b > 1
    mod_map = (lambda i, j: (i, 0, 0)) if per_batch else (lambda i, j: (0, 0, 0))
    in_specs = [pl.BlockSpec((1, bl, d), lambda i, j: (i, j, 0)),
                pl.BlockSpec((1, d), lambda i, j: (0, 0)),
                pl.BlockSpec((1, 1, d), mod_map),
                pl.BlockSpec((1, 1, d), mod_map)]
    args = [x, norm_w.reshape(1, d), shift, scale]
    out_specs = [pl.BlockSpec((bl, d), lambda i, j: (i * nl + j, 0))]
    out_shape = [jax.ShapeDtypeStruct((b * L, d), _BF16)]
    with_router = router is not None
    if with_router:
        in_specs.append(pl.BlockSpec((d, V7X_LANES), lambda i, j: (0, 0)))
        args.append(router)
        out_specs.append(pl.BlockSpec((bl, V7X_LANES), lambda i, j: (i * nl + j, 0)))
        out_shape.append(jax.ShapeDtypeStruct((b * L, V7X_LANES), _F32))
    out = pl.pallas_call(
        functools.partial(_modnorm_kernel, with_router=with_router),
        grid=(b, nl),
        in_specs=in_specs,
        out_specs=out_specs,
        out_shape=out_shape,
        compiler_params=pltpu.CompilerParams(
            dimension_semantics=("parallel", "parallel"),
            vmem_limit_bytes=_vmem_limit(2 * bl * d * 4, 2 * bl * d * 2, 4 * bl * d * 4, 2 * d * V7X_LANES * 4)),
        name="modnorm",
    )(*args)
    return out if with_router else out[0]


def _final_norm_kernel(x_ref, w_ref, o_ref):
    x = x_ref[...]
    o_ref[...] = x * lax.rsqrt(jnp.mean(x * x, axis=-1, keepdims=True) + EPS) * w_ref[...]


def _final_norm(x, w, bl=256):
    m, d = x.shape
    return pl.pallas_call(
        _final_norm_kernel,
        grid=(m // bl,),
        in_specs=[pl.BlockSpec((bl, d), lambda i: (i, 0)), pl.BlockSpec((1, d), lambda i: (0, 0))],
        out_specs=pl.BlockSpec((bl, d), lambda i: (i, 0)),
        out_shape=jax.ShapeDtypeStruct((m, d), _F32),
        compiler_params=pltpu.CompilerParams(dimension_semantics=("parallel",),
                                             vmem_limit_bytes=_vmem_limit(6 * bl * d * 4)),
        name="final_norm",
    )(x, w.reshape(1, d))


_O_HF, _O_HB, _O_HGI, _O_MLK, _O_MLV = 0, 1024, 2048, 3072, 5120
_O_HGQ, _O_HGOG, _O_MLQ, _O_MLOG, _O_HY, _O_MERGE = 7168, 8192, 9216, 11264, 13312, 16384

HG_DK = HG_WIDTH // HG_HEADS
HG_CHUNK = 64
HG_SUB = 16
HG_BLOCK = 256
ML_CHUNK = 256


def _log_sigmoid(x):
    return jnp.minimum(x, 0.0) - jnp.log1p(jnp.exp(-jnp.abs(x)))


def _silu(x):
    return x * jax.nn.sigmoid(x)


def _logaddexp(a, b):
    return jnp.maximum(a, b) + jnp.log1p(jnp.exp(-jnp.abs(a - b)))


def _hgrn2_kernel(*refs, rev, with_out, finalize, has_init, n_chunks):
    refs = list(refs)
    pre_ref, v_ref = refs.pop(0), refs.pop(0)
    q_ref = refs.pop(0) if with_out else None
    lbc_ref = refs.pop(0)
    s0_ref = refs.pop(0) if has_init else None
    if finalize:
        ofwd_ref, og_ref, nw_ref = refs.pop(0), refs.pop(0), refs.pop(0)
    o_ref = refs.pop(0) if with_out else None
    st_ref = refs.pop(0)
    C, SUB = HG_CHUNK, HG_SUB
    n_sub = C // SUB

    @pl.when(pl.program_id(2) == 0)
    def _():
        st_ref[0, 0] = s0_ref[0, 0] if has_init else jnp.zeros(st_ref.shape[2:], _F32)

    lbc = lbc_ref[...]
    log_lb, log_1mlb, one_m_lb = lbc[0:1], lbc[1:2], lbc[2:3]
    r_i = lax.broadcasted_iota(jnp.int32, (C, C), 0)
    c_i = lax.broadcasted_iota(jnp.int32, (C, C), 1)
    tri = ((c_i >= r_i) if rev else (c_i <= r_i)).astype(_F32)
    row64 = lax.broadcasted_iota(jnp.int32, (C, 1), 0)
    row16 = lax.broadcasted_iota(jnp.int32, (SUB, 1), 0)
    lane64 = lax.broadcasted_iota(jnp.int32, (SUB, C), 1)
    last = 0 if rev else C - 1

    def chunk_terms(c):
        rows = slice(c * C, (c + 1) * C)
        pre = pre_ref[0, rows, :].astype(_F32)
        v = v_ref[0, rows, :]
        g = _logaddexp(log_lb, log_1mlb + _log_sigmoid(pre))
        kk = one_m_lb * jax.nn.sigmoid(-pre)
        b = jnp.dot(tri, g, preferred_element_type=_F32, precision=lax.Precision.HIGHEST)
        b_end = b[last:last + 1]
        kt = (kk * jnp.exp(b_end - b)).astype(_BF16)
        upd = lax.dot_general(v, kt, (((0,), (0,)), ((), ())), preferred_element_type=_F32)
        if not with_out:
            return jnp.exp(b_end), upd, None, None
        q = _silu(q_ref[0, rows, :].astype(_F32))
        qe = (q * jnp.exp(b)).astype(_BF16)
        bx = b - g
        atts = []
        for i in range(n_sub):
            sub = slice(i * SUB, (i + 1) * SUB)
            first = i * SUB + (SUB - 1 if rev else 0)
            piv = bx[first:first + 1]
            qi = q[sub] * jnp.exp(b[sub] - piv)
            earlier = (row64 >= (i + 1) * SUB) if rev else (row64 < i * SUB)
            khat = jnp.where(earlier, kk * jnp.exp(jnp.where(earlier, piv - b, 0.0)), 0.0)
            att = lax.dot_general(qi.astype(_BF16), khat.astype(_BF16),
                                  (((1,), (1,)), ((), ())), preferred_element_type=_F32)
            for s in range(SUB):
                r = i * SUB + s
                valid = (row16 <= s) if rev else (row16 >= s)
                rel = jnp.where(valid, b[sub] - b[r:r + 1], 0.0)
                z = jnp.where(valid, q[sub] * jnp.exp(rel) * kk[r:r + 1], 0.0)
                att = jnp.where(lane64 == r, jnp.sum(z, axis=1, keepdims=True), att)
            atts.append(att)
        att = jnp.concatenate(atts, axis=0).astype(_BF16)
        return jnp.exp(b_end), upd, qe, jnp.dot(att, v, preferred_element_type=_F32)

    terms = [chunk_terms(c) for c in range(n_chunks)]
    st = st_ref[0, 0]
    outs = [None] * n_chunks
    for c in (reversed(range(n_chunks)) if rev else range(n_chunks)):
        dec, upd, qe, o_intra = terms[c]
        if with_out:
            outs[c] = o_intra + lax.dot_general(qe, st.astype(_BF16), (((1,), (1,)), ((), ())),
                                                preferred_element_type=_F32)
        st = st * dec + upd
    st_ref[0, 0] = st
    if with_out:
        o = jnp.concatenate(outs, axis=0)
        if finalize:
            y = ofwd_ref[0] + o
            y = y * lax.rsqrt(jnp.mean(y * y, axis=-1, keepdims=True) + EPS) * nw_ref[...]
            o_ref[0] = (y * _silu(og_ref[0].astype(_F32))).astype(o_ref.dtype)
        else:
            o_ref[0] = o


def _hgrn2(p, lbc, *, d, with_out, init=None, o_fwd=None, norm_w=None):
    bsz, L, _ = p.shape
    rev = d == 1
    tb = min(L, HG_BLOCK)
    assert L % tb == 0 and tb % HG_CHUNK == 0
    nblk = L // tb
    finalize = o_fwd is not None
    has_init = init is not None
    blk = (lambda n: nblk - 1 - n) if rev else (lambda n: n)
    pre_off = (_O_HB if rev else _O_HF) // HG_DK

    def tok_spec(off):
        return pl.BlockSpec((1, tb, HG_DK), lambda b, h, n: (b, blk(n), off + h))

    in_specs = [tok_spec(pre_off), tok_spec(_O_HGI // HG_DK)]
    args = [p, p]
    if with_out:
        in_specs.append(tok_spec(_O_HGQ // HG_DK))
        args.append(p)
    in_specs.append(pl.BlockSpec((None, 8, HG_DK), lambda b, h, n: (d, 0, h)))
    args.append(lbc)
    state_spec = pl.BlockSpec((1, 1, HG_DK, HG_DK), lambda b, h, n: (b, h, 0, 0))
    if has_init:
        in_specs.append(state_spec)
        args.append(init)
    if finalize:
        in_specs += [tok_spec(0), tok_spec(_O_HGOG // HG_DK), pl.BlockSpec((1, HG_DK), lambda b, h, n: (0, h))]
        args += [o_fwd, p, norm_w.reshape(1, HG_WIDTH)]
    out_specs, out_shape = [], []
    if with_out:
        out_specs.append(tok_spec(0))
        out_shape.append(jax.ShapeDtypeStruct((bsz, L, HG_WIDTH), _BF16 if finalize else _F32))
    out_specs.append(state_spec)
    out_shape.append(jax.ShapeDtypeStruct((bsz, HG_HEADS, HG_DK, HG_DK), _F32))
    out = pl.pallas_call(
        functools.partial(_hgrn2_kernel, rev=rev, with_out=with_out, finalize=finalize, has_init=has_init,
                          n_chunks=tb // HG_CHUNK),
        grid=(bsz, HG_HEADS, nblk),
        in_specs=in_specs, out_specs=out_specs, out_shape=out_shape,
        compiler_params=pltpu.CompilerParams(dimension_semantics=("parallel", "parallel", "arbitrary")),
        name="hgrn2",
    )(*args)
    return (out[0], out[1]) if with_out else (None, out[0])


def _mlstm_kernel(*refs, rev, with_out, finalize, has_init, seg):
    refs = list(refs)
    k_ref, v_ref = refs.pop(0), refs.pop(0)
    q_ref = refs.pop(0) if with_out else None
    g_ref, ckw_ref, ckb_ref = refs.pop(0), refs.pop(0), refs.pop(0)
    if with_out:
        cqw_ref, cqb_ref = refs.pop(0), refs.pop(0)
    if has_init:
        c0_ref, n0_ref, m0_ref = refs.pop(0), refs.pop(0), refs.pop(0)
    if finalize:
        ofwd_ref, og_ref, nw_ref = refs.pop(0), refs.pop(0), refs.pop(0)
    o_ref = refs.pop(0) if with_out else None
    c_ref, n_ref, m_ref = refs
    C = k_ref.shape[1]

    @pl.when(pl.program_id(2) == 0)
    def _():
        if has_init:
            c_ref[...] = c0_ref[...]
            n_ref[...] = n0_ref[...]
            m_ref[...] = m0_ref[...]
        else:
            c_ref[...] = jnp.zeros(c_ref.shape, _F32)
            n_ref[...] = jnp.zeros(n_ref.shape, _F32)
            m_ref[...] = jnp.zeros(m_ref.shape, _F32)

    g = g_ref[0, 0]
    ji = 2 if rev else 0
    i_row = g[ji:ji + 1]
    f_row = _log_sigmoid(g[ji + 1:ji + 2])
    r_i = lax.broadcasted_iota(jnp.int32, (C, C), 0)
    c_i = lax.broadcasted_iota(jnp.int32, (C, C), 1)
    mask = (c_i >= r_i) if rev else (c_i <= r_i)
    upto = ((r_i >= c_i) if rev else (r_i <= c_i)).astype(_F32)
    b_row = jnp.dot(jnp.broadcast_to(f_row, (8, C)), upto, preferred_element_type=_F32,
                    precision=lax.Precision.HIGHEST)[0:1]
    b_col = jnp.sum(jnp.where(mask, f_row, 0.0), axis=1, keepdims=True)
    i_col = jnp.sum(jnp.where(r_i == c_i, i_row, 0.0), axis=1, keepdims=True)
    b_end = jnp.sum(f_row, axis=1, keepdims=True)
    m_prev = m_ref[0, 0][0:1, 0:1]
    cm = c_ref[0, 0]
    nrow = n_ref[0, 0]

    pos = lax.broadcasted_iota(jnp.int32, (C, 1), 0) % seg

    def conv(x_ref, w_ref, b_ref):
        x = x_ref[0].astype(_F32)
        w = w_ref[...]
        x_prev = jnp.where(pos == 0, 0.0, pltpu.roll(x, 1, 0))
        x_next = jnp.where(pos == seg - 1, 0.0, pltpu.roll(x, C - 1, 0))
        return b_ref[...] + w[0:1] * x_prev + w[1:2] * x + w[2:3] * x_next

    k = _silu(conv(k_ref, ckw_ref, ckb_ref)) * ML_DH ** -0.5
    v = v_ref[0]

    if with_out:
        q = _silu(conv(q_ref, cqw_ref, cqb_ref))
        qb = q.astype(_BF16)
        logd = b_col - b_row + i_row
        m_t = jnp.maximum(b_col + m_prev, jnp.max(jnp.where(mask, logd, NEG_BIG), axis=1, keepdims=True))
        prev = jnp.exp(b_col + m_prev - m_t)
        dmat = jnp.where(mask, jnp.exp(jnp.where(mask, logd - m_t, 0.0)), 0.0)
        s = lax.dot_general(qb, k.astype(_BF16), (((1,), (1,)), ((), ())), preferred_element_type=_F32) * dmat
        num = prev * jnp.dot(qb, cm.astype(_BF16), preferred_element_type=_F32) \
            + jnp.dot(s.astype(_BF16), v, preferred_element_type=_F32)
        den = prev * jnp.sum(q * nrow, axis=1, keepdims=True) + jnp.sum(s, axis=1, keepdims=True)
        h = num / jnp.maximum(jnp.abs(den), jnp.exp(-m_t))
        if finalize:
            y = ofwd_ref[0] + h
            y = y * lax.rsqrt(jnp.mean(y * y, axis=-1, keepdims=True) + EPS) * nw_ref[...]
            o_ref[0] = (y * jax.nn.sigmoid(og_ref[0].astype(_F32))).astype(o_ref.dtype)
        else:
            o_ref[0] = h

    log_w = b_end - b_col + i_col
    m_new = jnp.maximum(b_end + m_prev, jnp.max(log_w, axis=0, keepdims=True))
    decay = jnp.exp(b_end + m_prev - m_new)
    wk = jnp.exp(log_w - m_new) * k
    c_ref[0, 0] = decay * cm + lax.dot_general(wk.astype(_BF16), v, (((0,), (0,)), ((), ())),
                                               preferred_element_type=_F32)
    n_ref[0, 0] = decay * nrow + jnp.sum(wk, axis=0, keepdims=True)
    m_ref[0, 0] = jnp.broadcast_to(m_new, m_ref.shape[2:])


def _mlstm(p, gate_rows, conv_w, conv_b, layer, *, d, seg, with_out, init=None, o_fwd=None, norm_w=None):
    bsz, L, _ = p.shape
    rev = d == 1
    C = min(L, ML_CHUNK)
    assert L % C == 0 and C % seg == 0
    nch = L // C
    finalize = o_fwd is not None
    has_init = init is not None
    blk = (lambda n: nch - 1 - n) if rev else (lambda n: n)

    def tok_spec(off):
        return pl.BlockSpec((1, C, ML_DH), lambda b, h, n: (b, blk(n), off + h))

    def conv_specs(off):
        return [pl.BlockSpec((None, SHORT_CONV, ML_DH), lambda b, h, n: (layer, 0, off + h)),
                pl.BlockSpec((None, 1, ML_DH), lambda b, h, n: (layer, 0, off + h))]

    conv_b3 = conv_b.reshape(DEPTH, 1, 2 * ML_WIDTH)
    in_specs = [tok_spec(_O_MLK // ML_DH), tok_spec(_O_MLV // ML_DH)]
    args = [p, p]
    if with_out:
        in_specs.append(tok_spec(_O_MLQ // ML_DH))
        args.append(p)
    in_specs.append(pl.BlockSpec((1, 1, 8, C), lambda b, h, n: (b, h, 0, blk(n))))
    args.append(gate_rows)
    in_specs += conv_specs(0)
    args += [conv_w, conv_b3]
    if with_out:
        in_specs += conv_specs(ML_HEADS)
        args += [conv_w, conv_b3]
    state_specs = [pl.BlockSpec((1, 1, ML_DH, ML_DH), lambda b, h, n: (b, h, 0, 0)),
                   pl.BlockSpec((1, 1, 1, ML_DH), lambda b, h, n: (b, h, 0, 0)),
                   pl.BlockSpec((1, 1, 8, V7X_LANES), lambda b, h, n: (b, h, 0, 0))]
    state_shape = [jax.ShapeDtypeStruct((bsz, ML_HEADS, ML_DH, ML_DH), _F32),
                   jax.ShapeDtypeStruct((bsz, ML_HEADS, 1, ML_DH), _F32),
                   jax.ShapeDtypeStruct((bsz, ML_HEADS, 8, V7X_LANES), _F32)]
    if has_init:
        in_specs += state_specs
        args += list(init)
    if finalize:
        in_specs += [tok_spec(0), tok_spec(_O_MLOG // ML_DH), pl.BlockSpec((1, ML_DH), lambda b, h, n: (0, h))]
        args += [o_fwd, p, norm_w.reshape(1, ML_WIDTH)]
    out_specs, out_shape = [], []
    if with_out:
        out_specs.append(tok_spec(0))
        out_shape.append(jax.ShapeDtypeStruct((bsz, L, ML_WIDTH), _BF16 if finalize else _F32))
    out = pl.pallas_call(
        functools.partial(_mlstm_kernel, rev=rev, with_out=with_out, finalize=finalize, has_init=has_init, seg=seg),
        grid=(bsz, ML_HEADS, nch),
        in_specs=in_specs, out_specs=out_specs + state_specs, out_shape=out_shape + state_shape,
        compiler_params=pltpu.CompilerParams(dimension_semantics=("parallel", "parallel", "arbitrary")),
        name="mlstm",
    )(*args)
    return (out[0], tuple(out[1:])) if with_out else (None, tuple(out))


def _dft_kernel(c_ref, s_ref, st_ref, *, n_fft, bm):
    shape = c_ref.shape
    row = lax.broadcasted_iota(jnp.int32, shape, 0) + pl.program_id(0) * bm
    col = lax.broadcasted_iota(jnp.int32, shape, 1)
    ang = ((row * col) & (n_fft - 1)).astype(_F32) * (2.0 * math.pi / n_fft)
    alt_col = jnp.where((col & 1) == 0, 1.0, -1.0)
    alt_row = jnp.where((row & 1) == 0, 1.0, -1.0)
    msin = -jnp.sin(ang)
    c_ref[...] = jnp.cos(ang).astype(c_ref.dtype)
    s_ref[...] = jnp.where(row == 0, alt_col, msin).astype(s_ref.dtype)
    st_ref[...] = jnp.where(col == 0, alt_row, msin).astype(st_ref.dtype)


def _dft_mats(L, bm=256):
    bm = min(bm, L)
    spec = pl.BlockSpec((bm, L), lambda i: (i, 0))
    shape = jax.ShapeDtypeStruct((L, L), _BF16)
    return pl.pallas_call(
        functools.partial(_dft_kernel, n_fft=2 * L, bm=bm),
        grid=(L // bm,), in_specs=[], out_specs=[spec] * 3, out_shape=[shape] * 3,
        compiler_params=pltpu.CompilerParams(dimension_semantics=("parallel",)),
        name="dft_mats",
    )()


def _hy_filter_kernel(band_ref, w1t_ref, w1c_ref, w1s_ref, b1_ref, w2_ref, b2_ref, w3_ref, fr_ref, dl_ref,
                      klo_ref, khi_ref, asum_ref, *, L, bl):
    hp = lax.Precision.HIGHEST
    m = (lax.broadcasted_iota(jnp.int32, (bl, 1), 0) + pl.program_id(0) * bl)
    half = HY_ORDER * HY_WIDTH
    total = jnp.zeros((1, half), _F32)
    for side, out_ref in ((0, klo_ref), (1, khi_ref)):
        pos = (m if side == 0 else (L - 1 - m)).astype(_F32)
        t = pos / (L - 1)
        ang = (2.0 * math.pi / L) * pos * band_ref[...]
        a = t * w1t_ref[...] + jnp.dot(jnp.cos(ang), w1c_ref[...], precision=hp, preferred_element_type=_F32) \
            + jnp.dot(jnp.sin(ang), w1s_ref[...], precision=hp, preferred_element_type=_F32) + b1_ref[...]
        a = jnp.sin(fr_ref[0:1] * a)
        a = jnp.sin(fr_ref[1:2] * (jnp.dot(a, w2_ref[...], precision=hp, preferred_element_type=_F32) + b2_ref[...]))
        f = jnp.dot(a, w3_ref[:, side * half:(side + 1) * half], precision=hp, preferred_element_type=_F32)
        f = f * jnp.exp(-t * dl_ref[...])
        if side == 1:
            f = jnp.where(m == 0, 0.0, f)
        out_ref[...] = f.astype(out_ref.dtype)
        total = total + jnp.sum(jnp.abs(f), axis=0, keepdims=True)

    @pl.when(pl.program_id(0) == 0)
    def _():
        asum_ref[...] = jnp.zeros(asum_ref.shape, _F32)

    asum_ref[...] += jnp.broadcast_to(total, asum_ref.shape)


def _hy_filter(L, w1, b1, w2, b2, w3, freq, bl=256):
    bl = min(bl, L)
    ffn = w1.shape[1]
    half = HY_ORDER * HY_WIDTH
    bands = jnp.pad(jnp.linspace(1e-4, HY_BANDS - 1, HY_BANDS, dtype=_F32), (0, V7X_LANES - HY_BANDS)).reshape(1, -1)
    pad_rows = ((0, V7X_LANES - HY_BANDS), (0, 0))
    deltas = jnp.abs(jnp.linspace(HY_MIN_DECAY, HY_MAX_DECAY, HY_WIDTH, dtype=_F32))
    args = [bands, w1[0:1], jnp.pad(w1[1:1 + HY_BANDS], pad_rows), jnp.pad(w1[1 + HY_BANDS:], pad_rows),
            b1.reshape(1, ffn), w2, b2.reshape(1, ffn), w3, freq, jnp.tile(deltas, HY_ORDER).reshape(1, half)]
    full = lambda a: pl.BlockSpec(a.shape, lambda i: (0,) * a.ndim)
    return pl.pallas_call(
        functools.partial(_hy_filter_kernel, L=L, bl=bl),
        grid=(L // bl,),
        in_specs=[full(a) for a in args],
        out_specs=[pl.BlockSpec((bl, half), lambda i: (i, 0)), pl.BlockSpec((bl, half), lambda i: (i, 0)),
                   pl.BlockSpec((8, half), lambda i: (0, 0))],
        out_shape=[jax.ShapeDtypeStruct((L, half), _BF16), jax.ShapeDtypeStruct((L, half), _BF16),
                   jax.ShapeDtypeStruct((8, half), _F32)],
        compiler_params=pltpu.CompilerParams(dimension_semantics=("arbitrary",)),
        name="hy_filter",
    )(*args)


def _hy_kf_kernel(t_ref, klo_ref, khi_ref, asum_ref, o_ref, *, bm):
    row = lax.broadcasted_iota(jnp.int32, (bm, 1), 0) + pl.program_id(0) * bm
    sgn = jnp.where((row & 1) == 0, 1.0, -1.0)
    t = t_ref[...]
    acc = jnp.dot(t, klo_ref[...], preferred_element_type=_F32) \
        + sgn * jnp.dot(t, khi_ref[...], preferred_element_type=_F32)
    o_ref[...] = acc / (asum_ref[0:1] + EPS)


def _hy_kf(cmat, smat, klo, khi, asum, bm=512, bn=512):
    L = cmat.shape[0]
    tmat = jnp.concatenate([cmat, smat], axis=0)
    half = klo.shape[1]
    bm, bn = min(bm, L), min(bn, half)
    return pl.pallas_call(
        functools.partial(_hy_kf_kernel, bm=bm),
        grid=(2 * L // bm, half // bn),
        in_specs=[pl.BlockSpec((bm, L), lambda i, j: (i, 0)),
                  pl.BlockSpec((L, bn), lambda i, j: (0, j)), pl.BlockSpec((L, bn), lambda i, j: (0, j)),
                  pl.BlockSpec((8, bn), lambda i, j: (0, j))],
        out_specs=pl.BlockSpec((bm, bn), lambda i, j: (i, j)),
        out_shape=jax.ShapeDtypeStruct((2 * L, half), _F32),
        compiler_params=pltpu.CompilerParams(
            dimension_semantics=("parallel", "parallel"),
            vmem_limit_bytes=_vmem_limit(2 * bm * L * 2, 4 * L * bn * 2, 6 * bm * bn * 4)),
        name="hy_kf",
    )(tmat, klo, khi, asum)


def _hy_split_kernel(*refs, seg):
    p_refs, w_refs, b_refs, o_refs = refs[0:3], refs[3:6], refs[6:9], refs[9:12]
    bl = o_refs[0].shape[1]
    pos = lax.broadcasted_iota(jnp.int32, (bl, 1), 0) % seg
    for p_ref, w_ref, b_ref, o_ref in zip(p_refs, w_refs, b_refs, o_refs):
        x = p_ref[0].astype(_F32)
        w = w_ref[...]
        x_prev = jnp.where(pos == 0, 0.0, pltpu.roll(x, 1, 0))
        x_next = jnp.where(pos == seg - 1, 0.0, pltpu.roll(x, bl - 1, 0))
        o_ref[0] = (b_ref[...] + w[0:1] * x_prev + w[1:2] * x + w[2:3] * x_next).astype(o_ref.dtype)


def _hy_split(p, conv_w, conv_b, seg):
    bsz, L, _ = p.shape
    bl = min(L, 256)
    assert L % bl == 0 and bl % seg == 0
    base = _O_HY // HY_WIDTH
    out_spec = pl.BlockSpec((1, bl, HY_WIDTH), lambda b, i: (b, i, 0))
    conv_b2 = conv_b.reshape(1, 3 * HY_WIDTH)
    in_specs = [pl.BlockSpec((1, bl, HY_WIDTH), lambda b, i, j=j: (b, i, base + j)) for j in range(3)]
    in_specs += [pl.BlockSpec((SHORT_CONV, HY_WIDTH), lambda b, i, j=j: (0, j)) for j in range(3)]
    in_specs += [pl.BlockSpec((1, HY_WIDTH), lambda b, i, j=j: (0, j)) for j in range(3)]
    return pl.pallas_call(
        functools.partial(_hy_split_kernel, seg=seg),
        grid=(bsz, L // bl),
        in_specs=in_specs,
        out_specs=[out_spec] * 3,
        out_shape=[jax.ShapeDtypeStruct((bsz, L, HY_WIDTH), _BF16)] * 3,
        compiler_params=pltpu.CompilerParams(dimension_semantics=("parallel", "parallel")),
        name="hy_split",
    )(p, p, p, conv_w, conv_w, conv_w, conv_b2, conv_b2, conv_b2)


def _hy_fwd_kernel(c_ref, s_ref, z_ref, kre_ref, kim_ref, pre_ref, pim_ref, *, bm):
    z = z_ref[...]
    xre = jnp.dot(c_ref[...], z, preferred_element_type=_F32)
    xim = jnp.dot(s_ref[...], z, preferred_element_type=_F32)
    kre, kim = kre_ref[...], kim_ref[...]
    row0 = (lax.broadcasted_iota(jnp.int32, (bm, 1), 0) + pl.program_id(0) * bm) == 0
    pre = jnp.where(row0, 0.5 * xre * kre, xre * kre - xim * kim)
    pim = jnp.where(row0, 0.5 * xim * kim, xre * kim + xim * kre)
    pre_ref[...] = pre.astype(pre_ref.dtype)
    pim_ref[...] = pim.astype(pim_ref.dtype)


def _hy_fwd(cmat, smat, z, kf, order, bm=512, bn=512):
    bsz, L, width = z.shape
    bm, bn = min(bm, L), min(bn, width)
    nj = width // bn
    mat_spec = pl.BlockSpec((bm, L), lambda i, b, j: (i, 0))
    out_spec = pl.BlockSpec((None, bm, bn), lambda i, b, j: (b, i, j))
    out_shape = jax.ShapeDtypeStruct((bsz, L, width), _BF16)
    return pl.pallas_call(
        functools.partial(_hy_fwd_kernel, bm=bm),
        grid=(L // bm, bsz, nj),
        in_specs=[mat_spec, mat_spec,
                  pl.BlockSpec((None, L, bn), lambda i, b, j: (b, 0, j)),
                  pl.BlockSpec((bm, bn), lambda i, b, j: (i, order * nj + j)),
                  pl.BlockSpec((bm, bn), lambda i, b, j: (L // bm + i, order * nj + j))],
        out_specs=[out_spec, out_spec], out_shape=[out_shape, out_shape],
        compiler_params=pltpu.CompilerParams(
            dimension_semantics=("parallel", "parallel", "parallel"),
            vmem_limit_bytes=_vmem_limit(4 * bm * L * 2, 2 * L * bn * 2, 4 * bm * bn * 4, 4 * bm * bn * 2,
                                         4 * bm * bn * 4)),
        name="hy_fwd",
    )(cmat, smat, z, kf, kf)


def _hy_inv_kernel(c_ref, st_ref, pre_ref, pim_ref, z_ref, g_ref, skip_ref, o_ref, *, scale):
    conv = jnp.dot(c_ref[...], pre_ref[...], preferred_element_type=_F32) \
        + jnp.dot(st_ref[...], pim_ref[...], preferred_element_type=_F32)
    z = z_ref[...].astype(_F32)
    o_ref[...] = (g_ref[...].astype(_F32) * (scale * conv + skip_ref[...] * z)).astype(o_ref.dtype)


def _hy_inv(cmat, smat_t, pre, pim, z, gate, skip, bm=512, bn=512):
    bsz, L, width = z.shape
    bm, bn = min(bm, L), min(bn, width)
    mat_spec = pl.BlockSpec((bm, L), lambda i, b, j: (i, 0))
    spec_spec = pl.BlockSpec((None, L, bn), lambda i, b, j: (b, 0, j))
    tok_spec = pl.BlockSpec((None, bm, bn), lambda i, b, j: (b, i, j))
    return pl.pallas_call(
        functools.partial(_hy_inv_kernel, scale=1.0 / L),
        grid=(L // bm, bsz, width // bn),
        in_specs=[mat_spec, mat_spec, spec_spec, spec_spec, tok_spec, tok_spec,
                  pl.BlockSpec((1, bn), lambda i, b, j: (0, j))],
        out_specs=tok_spec,
        out_shape=jax.ShapeDtypeStruct((bsz, L, width), _BF16),
        compiler_params=pltpu.CompilerParams(
            dimension_semantics=("parallel", "parallel", "parallel"),
            vmem_limit_bytes=_vmem_limit(4 * bm * L * 2, 4 * L * bn * 2, 6 * bm * bn * 2, 3 * bm * bn * 4)),
        name="hy_inv",
    )(cmat, smat_t, pre, pim, z, gate, skip.reshape(1, width))


def _hyena(p, lp, seg, dft):
    L = p.shape[1]
    cmat, smat, smat_t = dft
    klo, khi, asum = _hy_filter(L, lp["hy_w1"], lp["hy_b1"], lp["hy_w2"], lp["hy_b2"], lp["hy_w3"], lp["hy_freq"])
    kf = _hy_kf(cmat, smat, klo, khi, asum)
    z, x1, x2 = _hy_split(p, lp["hy_conv_w"], lp["hy_conv_b"], seg)
    for o, gate in enumerate((x1, x2)):
        pre, pim = _hy_fwd(cmat, smat, z, kf, o)
        z = _hy_inv(cmat, smat_t, pre, pim, z, gate, lp["hy_skip"][o])
    return z


def _prep_layer_weights(l, w_in, w_branch, w_out):
    wl = w_in[l]
    w_main = jnp.concatenate([wl[:, :N_STATE_MAIN], wl[:, N_STATE_COLS:]], axis=1).astype(_BF16)
    w_gate = jnp.pad(wl[:, N_STATE_MAIN:N_STATE_COLS], ((0, 0), (0, V7X_LANES - N_GATE_COLS))).astype(_BF16)
    return w_main, w_gate, w_branch[l].astype(_BF16), w_out[l].astype(_BF16)


def _lower_bound_rows(lb):
    lb = lb.astype(_F32)
    rows = jnp.stack([jnp.log(jnp.maximum(lb, LB_FLOOR)), jnp.log1p(-lb), 1.0 - lb], axis=1)
    return jnp.pad(rows, ((0, 0), (0, 5), (0, 0)))


def _gate_rows(gates, gate_b):
    bsz, L, _ = gates.shape
    g = (gates[..., :N_GATE_COLS] + gate_b).reshape(bsz, L, 4, ML_HEADS)
    return jnp.pad(g.transpose(0, 3, 2, 1), ((0, 0), (0, 0), (0, 4), (0, 0)))


def _stream_mixer(xs, mods, lw, lp, layer, n_rows, init, with_out):
    w_main, w_gate, wb, wo = lw
    bsz, L, d = xs.shape
    sh, sc, g = mods[0], mods[1], mods[2]
    h = _modnorm(xs, lp["norm1"], sh, sc)
    w_proj = w_main if with_out else w_main[:, :N_STATE_MAIN]
    p = _matmul(h, w_proj, out_dtype=_BF16).reshape(bsz, L, -1)
    gates = _matmul(h, w_gate, out_dtype=_F32).reshape(bsz, L, V7X_LANES)
    grows = _gate_rows(gates, lp["ml_gate_b"])
    seg = L // n_rows
    hg_init, ml_init = init if init is not None else ((None, None), (None, None))
    hg_kw = dict(with_out=with_out)
    ml_kw = dict(seg=seg, with_out=with_out)
    o_hg, s_hg_f = _hgrn2(p, lp["lbc"], d=0, init=hg_init[0], **hg_kw)
    y_hg, s_hg_b = _hgrn2(p, lp["lbc"], d=1, init=hg_init[1], o_fwd=o_hg, norm_w=lp["hg_norm_w"], **hg_kw)
    o_ml, s_ml_f = _mlstm(p, grows, lp["ml_conv_w"], lp["ml_conv_b"], layer, d=0, init=ml_init[0], **ml_kw)
    y_ml, s_ml_b = _mlstm(p, grows, lp["ml_conv_w"], lp["ml_conv_b"], layer, d=1, init=ml_init[1],
                          o_fwd=o_ml, norm_w=lp["ml_norm_w"], **ml_kw)
    finals = ((s_hg_f, s_hg_b), (s_ml_f, s_ml_b))
    if not with_out:
        return None, finals
    y_hy = _hyena(p, lp, seg, lp["dft"][L])
    m = bsz * L
    merge_p = p[..., _O_MERGE:].astype(_F32).reshape(m, N_BRANCH * d)
    g_hg, g_ml, g_hy = jnp.split(jax.nn.sigmoid(merge_p), N_BRANCH, axis=-1)
    merged = (g_hg * _matmul(y_hg.reshape(m, -1), wb[:HG_WIDTH], out_dtype=_F32)
              + g_ml * _matmul(y_ml.reshape(m, -1), wb[HG_WIDTH:HG_WIDTH + ML_WIDTH], out_dtype=_F32)
              + g_hy * _matmul(y_hy.reshape(m, -1), wb[HG_WIDTH + ML_WIDTH:], out_dtype=_F32))
    x_new = _matmul(merged.astype(_BF16), wo, out_dtype=_F32, res=xs.reshape(m, d), gate=g,
                    rows_per_gate=L if g.shape[0] > 1 else m)
    return x_new.reshape(bsz, L, d), finals


def _stream_ffn(xs, mods, norm2, l, ffn):
    bsz, L, d = xs.shape
    m = bsz * L
    sh2, sc2, g2 = mods[3], mods[4], mods[5]
    rows_per_gate = L if g2.shape[0] > 1 else m
    if l % 2 == 0:
        wg, wu, wd = ffn["dense"]
        h = _modnorm(xs, norm2, sh2, sc2)
        hid = _swiglu(h, wg, wu)
        out = _matmul(hid, wd, out_dtype=_F32, bk=DENSE_FF_PAD // 4, res=xs.reshape(m, d), gate=g2,
                      rows_per_gate=rows_per_gate)
    else:
        router, wg, wu, wd = ffn["moe"]
        h, comb = _modnorm(xs, norm2, sh2, sc2, router=router)
        hid = _swiglu(h, wg, wu, comb=comb, cols_per_expert=EXPERT_FF)
        out = _matmul(hid, wd, out_dtype=_F32, bk=2048, res=xs.reshape(m, d), gate=g2,
                      rows_per_gate=rows_per_gate)
    return out.reshape(bsz, L, d)


def kernel(x, c, ctx, c_ctx, norm1_w, norm2_w, ada_w, ada_b, w_in, hg_lb_logits, hg_norm_w,
           ml_conv_w, ml_conv_b, ml_gate_b, ml_norm_w, hy_conv_w, hy_conv_b, hy_w1, hy_b1,
           hy_w2, hy_b2, hy_w3, hy_freq, hy_skip, w_branch, w_out, dense_w_gate, dense_w_up,
           dense_w_down, moe_router, moe_w_gate, moe_w_up, moe_w_down, final_norm_w):
    bsz, L, d = x.shape
    rows = L // GRID_W
    lb_w = jax.nn.softmax(hg_lb_logits.astype(_F32), axis=0)
    lower_bounds = jnp.cumsum(lb_w, axis=0) - lb_w[0]

    cond = jnp.zeros((8, d), _F32).at[:bsz].set(c).at[bsz].set(c_ctx)
    dft = {n: _dft_mats(n) for n in {L, ctx.shape[1]}}
    for l in range(DEPTH):
        lp = {
            "norm1": norm1_w[l], "hg_norm_w": hg_norm_w[l], "lbc": _lower_bound_rows(lower_bounds[l]), "dft": dft,
            "ml_conv_w": ml_conv_w, "ml_conv_b": ml_conv_b, "ml_gate_b": ml_gate_b[l],
            "ml_norm_w": ml_norm_w[l], "hy_conv_w": hy_conv_w[l], "hy_conv_b": hy_conv_b[l],
            "hy_w1": hy_w1[l], "hy_b1": hy_b1[l], "hy_w2": hy_w2[l], "hy_b2": hy_b2[l],
            "hy_w3": hy_w3[l], "hy_freq": hy_freq[l], "hy_skip": hy_skip[l],
        }
        lw = _prep_layer_weights(l, w_in, w_branch, w_out)
        if l % 2 == 0:
            i = l // 2
            pad = DENSE_FF_PAD - DENSE_FF
            ffn = {"dense": (jnp.pad(dense_w_gate[i].astype(_BF16), ((0, 0), (0, pad))),
                             jnp.pad(dense_w_up[i].astype(_BF16), ((0, 0), (0, pad))),
                             jnp.pad(dense_w_down[i].astype(_BF16), ((0, pad), (0, 0))))}
        else:
            i = l // 2
            ffn = {"moe": (jnp.pad(moe_router[i], ((0, 0), (0, V7X_LANES - N_EXPERTS))),
                           moe_w_gate[i].astype(_BF16).transpose(1, 0, 2).reshape(d, N_EXPERTS * EXPERT_FF),
                           moe_w_up[i].astype(_BF16).transpose(1, 0, 2).reshape(d, N_EXPERTS * EXPERT_FF),
                           moe_w_down[i].astype(_BF16).reshape(N_EXPERTS * EXPERT_FF, d))}
        mod = _ada(cond, ada_w, ada_b, l)
        mods_lat = [mod[:bsz, j * d:(j + 1) * d].reshape(bsz, 1, d) for j in range(N_MOD)]
        mods_ctx = [mod[bsz:bsz + 1, j * d:(j + 1) * d].reshape(1, 1, d) for j in range(N_MOD)]
        last = l == DEPTH - 1
        ctx_new, ctx_fin = _stream_mixer(ctx, mods_ctx, lw, lp, l, 1, None, not last)
        if not last:
            ctx = _stream_ffn(ctx_new, mods_ctx, norm2_w[l], l, ffn)
        x, _ = _stream_mixer(x, mods_lat, lw, lp, l, rows, ctx_fin, True)
        x = _stream_ffn(x, mods_lat, norm2_w[l], l, ffn)
    return _final_norm(x.reshape(bsz * L, d), final_norm_w).reshape(bsz, L, d)
```

```python
import functools
import math

import jax
import jax.numpy as jnp
from jax import lax
from jax.experimental import pallas as pl
from jax.experimental.pallas import tpu as pltpu

D_MODEL = 4096
DEPTH = 2
GRID_W = 64
EPS = 1e-6
SHORT_CONV = 3
N_MOD = 6
NEG_BIG = -1e30
LB_FLOOR = 1e-30

HG_WIDTH = D_MODEL // 4
HG_HEADS = 8
ML_WIDTH = D_MODEL // 2
ML_HEADS = 8
ML_DH = ML_WIDTH // ML_HEADS
HY_WIDTH = D_MODEL // 4
HY_ORDER = 2
HY_POS_DIM = 33
HY_BANDS = (HY_POS_DIM - 1) // 2
HY_MIN_DECAY = math.log(1e-2) / 1.5
HY_MAX_DECAY = math.log(1e-2) / 0.3
N_BRANCH = 3

N_GATE_COLS = 4 * ML_HEADS
N_STATE_MAIN = 3 * HG_WIDTH + 2 * ML_WIDTH
N_STATE_COLS = N_STATE_MAIN + N_GATE_COLS
N_MAIN = N_STATE_MAIN + 2 * HG_WIDTH + 2 * ML_WIDTH + 3 * HY_WIDTH + N_BRANCH * D_MODEL

DENSE_FF = 11008
N_EXPERTS = 8
TOP_K = 2
EXPERT_FF = 2048

V7X_LANES = 128
V7X_VMEM_BYTES = 64 * 1024 * 1024
DENSE_FF_PAD = 11264

_F32 = jnp.float32
_BF16 = jnp.bfloat16


def _vmem_limit(*block_bytes):
    need = int(sum(block_bytes)) + (4 << 20)
    return min(max(need, 16 << 20), V7X_VMEM_BYTES - (6 << 20))


def _mm_kernel(*refs, nk, has_res):
    if has_res:
        x_ref, w_ref, res_ref, gate_ref = refs[:4]
        rest = refs[4:]
    else:
        x_ref, w_ref = refs[:2]
        rest = refs[2:]
    o_ref = rest[0]

    def finish(acc):
        if has_res:
            acc = res_ref[...] + gate_ref[0] * acc
        o_ref[...] = acc.astype(o_ref.dtype)

    part = jnp.dot(x_ref[...], w_ref[...], preferred_element_type=_F32)
    if nk == 1:
        finish(part)
        return
    acc_ref = rest[1]
    k = pl.program_id(2)

    @pl.when(k == 0)
    def _():
        acc_ref[...] = part

    @pl.when(k > 0)
    def _():
        acc_ref[...] += part

    @pl.when(k == nk - 1)
    def _():
        finish(acc_ref[...])


def _matmul(x, w, *, out_dtype, bm=1024, bn=1024, bk=None, res=None, gate=None, rows_per_gate=None):
    m, kdim = x.shape
    n = w.shape[1]
    bm, bn = min(bm, m), min(bn, n)
    bk = kdim if bk is None else bk
    assert m % bm == 0 and n % bn == 0 and kdim % bk == 0, (x.shape, w.shape, bm, bn, bk)
    nk = kdim // bk
    has_res = res is not None
    in_specs = [pl.BlockSpec((bm, bk), lambda i, j, k: (i, k)),
                pl.BlockSpec((bk, bn), lambda i, j, k: (k, j))]
    args = [x, w]
    osz = jnp.dtype(out_dtype).itemsize
    vm = [2 * bm * bk * 2, 2 * bk * bn * 2, 2 * bm * bn * osz, 2 * bm * bn * 4]
    if has_res:
        blocks_per_gate = rows_per_gate // bm
        assert rows_per_gate % bm == 0
        in_specs += [pl.BlockSpec((bm, bn), lambda i, j, k: (i, j)),
                     pl.BlockSpec((1, 1, bn), lambda i, j, k: (i // blocks_per_gate, 0, j))]
        args += [res, gate]
        vm.append(2 * bm * bn * 4)
    scratch = [pltpu.VMEM((bm, bn), _F32)] if nk > 1 else []
    return pl.pallas_call(
        functools.partial(_mm_kernel, nk=nk, has_res=has_res),
        grid=(m // bm, n // bn, nk),
        in_specs=in_specs,
        out_specs=pl.BlockSpec((bm, bn), lambda i, j, k: (i, j)),
        out_shape=jax.ShapeDtypeStruct((m, n), out_dtype),
        scratch_shapes=scratch,
        compiler_params=pltpu.CompilerParams(
            dimension_semantics=("parallel", "parallel", "arbitrary"),
            vmem_limit_bytes=_vmem_limit(*vm)),
        name="matmul",
    )(*args)


def _swiglu_kernel(x_ref, wg_ref, wu_ref, o_ref):
    x = x_ref[...]
    g = jnp.dot(x, wg_ref[...], preferred_element_type=_F32)
    u = jnp.dot(x, wu_ref[...], preferred_element_type=_F32)
    o_ref[...] = (g * jax.nn.sigmoid(g) * u).astype(o_ref.dtype)


def _swiglu(x, wg, wu, *, bm=1024, bn=512):
    m, kdim = x.shape
    n = wg.shape[1]
    bm = min(bm, m)
    assert m % bm == 0 and n % bn == 0
    return pl.pallas_call(
        _swiglu_kernel,
        grid=(m // bm, n // bn),
        in_specs=[pl.BlockSpec((bm, kdim), lambda i, j: (i, 0)),
                  pl.BlockSpec((kdim, bn), lambda i, j: (0, j)),
                  pl.BlockSpec((kdim, bn), lambda i, j: (0, j))],
        out_specs=pl.BlockSpec((bm, bn), lambda i, j: (i, j)),
        out_shape=jax.ShapeDtypeStruct((m, n), _BF16),
        compiler_params=pltpu.CompilerParams(
            dimension_semantics=("parallel", "parallel"),
            vmem_limit_bytes=_vmem_limit(2 * bm * kdim * 2, 4 * kdim * bn * 2, 2 * bm * bn * 2, 4 * bm * bn * 4)),
        name="swiglu",
    )(x, wg, wu)


MOE_TILE = 512


def _route_plan(route, bm):
    m = route.shape[0]
    e_flat = route[:, N_EXPERTS:N_EXPERTS + TOP_K].astype(jnp.int32).T.reshape(-1)
    p_flat = route[:, N_EXPERTS + TOP_K:N_EXPERTS + 2 * TOP_K].T.reshape(-1)
    onehot = (e_flat[:, None] == jnp.arange(N_EXPERTS)[None, :]).astype(jnp.int32)
    csum = jnp.cumsum(onehot, axis=0)
    rank = jnp.sum((csum - onehot) * onehot, axis=1)
    padded = (csum[-1] + bm - 1) // bm * bm
    ends = jnp.cumsum(padded)
    dest = ((ends - padded)[e_flat] + rank).astype(jnp.int32)
    n_rows = TOP_K * m + N_EXPERTS * bm
    src = jnp.zeros((n_rows,), jnp.int32).at[dest].set(jnp.tile(jnp.arange(m, dtype=jnp.int32), TOP_K))
    wrow = jnp.zeros((n_rows,), _F32).at[dest].set(p_flat)
    tile_start = jnp.arange(n_rows // bm, dtype=jnp.int32) * bm
    tile_e = jnp.minimum(jnp.sum(tile_start[:, None] >= ends[None, :], axis=1), N_EXPERTS - 1).astype(jnp.int32)
    n_used = (ends[-1] // bm).astype(jnp.int32).reshape(1)
    return src, wrow, tile_e, n_used, dest


def _gather_kernel(idx_ref, x_hbm, o_ref, sem, *, bm):
    base = pl.program_id(0) * bm

    def row_copy(r, src_row):
        return pltpu.make_async_copy(x_hbm.at[pl.ds(src_row, 1)], o_ref.at[pl.ds(r, 1)], sem)

    def issue(r, c):
        row_copy(r, idx_ref[base + r]).start()
        return c

    def drain(r, c):
        row_copy(r, 0).wait()
        return c

    lax.fori_loop(0, bm, issue, 0)
    lax.fori_loop(0, bm, drain, 0)


def _gather_rows(x, idx, bm=256):
    n = idx.shape[0]
    d = x.shape[1]
    assert n % bm == 0
    return pl.pallas_call(
        functools.partial(_gather_kernel, bm=bm),
        grid_spec=pltpu.PrefetchScalarGridSpec(
            num_scalar_prefetch=1, grid=(n // bm,),
            in_specs=[pl.BlockSpec(memory_space=pl.ANY)],
            out_specs=pl.BlockSpec((bm, d), lambda i, idx_ref: (i, 0)),
            scratch_shapes=[pltpu.SemaphoreType.DMA(())]),
        out_shape=jax.ShapeDtypeStruct((n, d), x.dtype),
        compiler_params=pltpu.CompilerParams(dimension_semantics=("arbitrary",)),
        name="gather_rows",
    )(idx, x)


def _gswiglu_kernel(te_ref, nu_ref, x_ref, wg_ref, wu_ref, w_ref, o_ref):
    @pl.when(pl.program_id(0) < nu_ref[0])
    def _():
        x = x_ref[...].astype(_BF16)
        g = jnp.dot(x, wg_ref[...], preferred_element_type=_F32)
        u = jnp.dot(x, wu_ref[...], preferred_element_type=_F32)
        o_ref[...] = (g * jax.nn.sigmoid(g) * u * w_ref[:, 0:1]).astype(o_ref.dtype)

    @pl.when(pl.program_id(0) >= nu_ref[0])
    def _():
        o_ref[...] = jnp.zeros(o_ref.shape, o_ref.dtype)


def _swiglu_grouped(xs, wg, wu, wrow, tile_e, n_used, bm, bn=512):
    n, d = xs.shape
    ff = wg.shape[2]
    w_spec = pl.BlockSpec((None, d, bn), lambda i, j, te, nu: (te[i], 0, j))
    return pl.pallas_call(
        _gswiglu_kernel,
        grid_spec=pltpu.PrefetchScalarGridSpec(
            num_scalar_prefetch=2, grid=(n // bm, ff // bn),
            in_specs=[pl.BlockSpec((bm, d), lambda i, j, te, nu: (i, 0)), w_spec, w_spec,
                      pl.BlockSpec((bm, V7X_LANES), lambda i, j, te, nu: (i, 0))],
            out_specs=pl.BlockSpec((bm, bn), lambda i, j, te, nu: (i, j))),
        out_shape=jax.ShapeDtypeStruct((n, ff), _BF16),
        compiler_params=pltpu.CompilerParams(
            dimension_semantics=("parallel", "arbitrary"),
            vmem_limit_bytes=_vmem_limit(2 * bm * d * 4, bm * d * 2, 4 * d * bn * 2, 2 * bm * bn * 2, 4 * bm * bn * 4)),
        name="swiglu_grouped",
    )(tile_e, n_used, xs, wg, wu, wrow)


def _gdown_kernel(te_ref, nu_ref, h_ref, w_ref, o_ref):
    @pl.when(pl.program_id(0) < nu_ref[0])
    def _():
        o_ref[...] = jnp.dot(h_ref[...], w_ref[...], preferred_element_type=_F32)

    @pl.when(pl.program_id(0) >= nu_ref[0])
    def _():
        o_ref[...] = jnp.zeros(o_ref.shape, o_ref.dtype)


def _down_grouped(hid, wd, tile_e, n_used, bm, bn=1024):
    n, ff = hid.shape
    d = wd.shape[2]
    return pl.pallas_call(
        _gdown_kernel,
        grid_spec=pltpu.PrefetchScalarGridSpec(
            num_scalar_prefetch=2, grid=(n // bm, d // bn),
            in_specs=[pl.BlockSpec((bm, ff), lambda i, j, te, nu: (i, 0)),
                      pl.BlockSpec((None, ff, bn), lambda i, j, te, nu: (te[i], 0, j))],
            out_specs=pl.BlockSpec((bm, bn), lambda i, j, te, nu: (i, j))),
        out_shape=jax.ShapeDtypeStruct((n, d), _F32),
        compiler_params=pltpu.CompilerParams(
            dimension_semantics=("parallel", "arbitrary"),
            vmem_limit_bytes=_vmem_limit(2 * bm * ff * 2, 2 * ff * bn * 2, 3 * bm * bn * 4)),
        name="down_grouped",
    )(tile_e, n_used, hid, wd)


def _combine_kernel(d_ref, y_hbm, x_ref, g_ref, o_ref, buf, sem, *, bt, m):
    base = pl.program_id(0) * bt

    def row_copy(k, r, src_row):
        return pltpu.make_async_copy(y_hbm.at[pl.ds(src_row, 1)], buf.at[k, pl.ds(r, 1)], sem)

    def issue(r, c):
        for k in range(TOP_K):
            row_copy(k, r, d_ref[k * m + base + r]).start()
        return c

    def drain(r, c):
        for k in range(TOP_K):
            row_copy(k, r, 0).wait()
        return c

    lax.fori_loop(0, bt, issue, 0)
    lax.fori_loop(0, bt, drain, 0)
    o_ref[...] = x_ref[...] + g_ref[0] * (buf[0] + buf[1])


def _combine(ys, dest, x, gate, rows_per_gate, bt=256):
    m, d = x.shape
    assert m % bt == 0 and rows_per_gate % bt == 0
    per = rows_per_gate // bt
    return pl.pallas_call(
        functools.partial(_combine_kernel, bt=bt, m=m),
        grid_spec=pltpu.PrefetchScalarGridSpec(
            num_scalar_prefetch=1, grid=(m // bt,),
            in_specs=[pl.BlockSpec(memory_space=pl.ANY),
                      pl.BlockSpec((bt, d), lambda i, dr: (i, 0)),
                      pl.BlockSpec((1, 1, d), lambda i, dr: (i // per, 0, 0))],
            out_specs=pl.BlockSpec((bt, d), lambda i, dr: (i, 0)),
            scratch_shapes=[pltpu.VMEM((TOP_K, bt, d), _F32), pltpu.SemaphoreType.DMA(())]),
        out_shape=jax.ShapeDtypeStruct((m, d), _F32),
        compiler_params=pltpu.CompilerParams(
            dimension_semantics=("arbitrary",),
            vmem_limit_bytes=_vmem_limit(TOP_K * bt * d * 4, 4 * bt * d * 4, 2 * bt * d * 4)),
        name="moe_combine",
    )(dest, ys, x, gate)


def _moe(h, route, x, gate, rows_per_gate, wg, wu, wd):
    src, wrow, tile_e, n_used, dest = _route_plan(route, MOE_TILE)
    xs = _gather_rows(h, src)
    wrow = jnp.broadcast_to(wrow[:, None], (wrow.shape[0], V7X_LANES))
    hid = _swiglu_grouped(xs, wg, wu, wrow, tile_e, n_used, MOE_TILE)
    ys = _down_grouped(hid, wd, tile_e, n_used, MOE_TILE)
    return _combine(ys, dest, x, gate, rows_per_gate)


def _merge_kernel(yhg_ref, yml_ref, yhy_ref, w0_ref, w1_ref, w2_ref, w3_ref, m0_ref, m1_ref, m2_ref, o_ref):
    half = yml_ref.shape[1] // 2
    a_hg = jnp.dot(yhg_ref[...], w0_ref[...], preferred_element_type=_F32)
    a_ml = jnp.dot(yml_ref[:, :half], w1_ref[...], preferred_element_type=_F32) \
        + jnp.dot(yml_ref[:, half:], w2_ref[...], preferred_element_type=_F32)
    a_hy = jnp.dot(yhy_ref[...], w3_ref[...], preferred_element_type=_F32)
    sig = lambda r: jax.nn.sigmoid(r[...].astype(_F32))
    o_ref[...] = (sig(m0_ref) * a_hg + sig(m1_ref) * a_ml + sig(m2_ref) * a_hy).astype(o_ref.dtype)


def _merge(y_hg, y_ml, y_hy, wb, p2d, merge_off, bm=1024, bn=512):
    m = y_hg.shape[0]
    d = wb.shape[1]
    bm = min(bm, m)
    kb = HG_WIDTH
    assert m % bm == 0 and d % bn == 0 and ML_WIDTH == 2 * kb and HY_WIDTH == kb and merge_off % bn == 0
    y_spec = lambda w: pl.BlockSpec((bm, w), lambda i, j: (i, 0))
    w_spec = lambda r: pl.BlockSpec((kb, bn), lambda i, j, r=r: (r, j))
    g_spec = lambda b: pl.BlockSpec((bm, bn), lambda i, j, b=b: (i, (merge_off + b * d) // bn + j))
    return pl.pallas_call(
        _merge_kernel,
        grid=(m // bm, d // bn),
        in_specs=[y_spec(HG_WIDTH), y_spec(ML_WIDTH), y_spec(HY_WIDTH)] + [w_spec(r) for r in range(4)]
        + [g_spec(b) for b in range(N_BRANCH)],
        out_specs=pl.BlockSpec((bm, bn), lambda i, j: (i, j)),
        out_shape=jax.ShapeDtypeStruct((m, d), _BF16),
        compiler_params=pltpu.CompilerParams(
            dimension_semantics=("parallel", "parallel"),
            vmem_limit_bytes=_vmem_limit(2 * bm * 4 * kb * 2, 8 * kb * bn * 2, 6 * bm * bn * 2, 2 * bm * bn * 2,
                                         4 * bm * bn * 4)),
        name="merge",
    )(y_hg, y_ml, y_hy, wb, wb, wb, wb, p2d, p2d, p2d)


def _ada_kernel(c_ref, w_ref, b_ref, o_ref):
    c = c_ref[...]
    a = (c * jax.nn.sigmoid(c)).astype(_BF16)
    o_ref[...] = jnp.dot(a, w_ref[...].astype(_BF16), preferred_element_type=_F32) + b_ref[0]


def _ada(cond, ada_w, ada_b, layer, bn=512):
    rows = cond.shape[0]
    n = N_MOD * D_MODEL
    return pl.pallas_call(
        _ada_kernel,
        grid=(n // bn,),
        in_specs=[pl.BlockSpec((rows, D_MODEL), lambda j: (0, 0)),
                  pl.BlockSpec((None, D_MODEL, bn), lambda j: (layer, 0, j)),
                  pl.BlockSpec((None, 1, bn), lambda j: (layer, 0, j))],
        out_specs=pl.BlockSpec((rows, bn), lambda j: (0, j)),
        out_shape=jax.ShapeDtypeStruct((rows, n), _F32),
        compiler_params=pltpu.CompilerParams(
            dimension_semantics=("parallel",),
            vmem_limit_bytes=_vmem_limit(2 * D_MODEL * bn * 4, D_MODEL * bn * 2)),
        name="ada",
    )(cond, ada_w, ada_b.reshape(DEPTH, 1, n))


def _modnorm_kernel(*refs, with_router):
    if with_router:
        x_ref, w_ref, sh_ref, sc_ref, r_ref, h_ref, comb_ref = refs
    else:
        x_ref, w_ref, sh_ref, sc_ref, h_ref = refs
    x = x_ref[0]
    y = x * lax.rsqrt(jnp.mean(x * x, axis=-1, keepdims=True) + EPS)
    h = (y * w_ref[0]) * (1.0 + sc_ref[0]) + sh_ref[0]
    h_ref[...] = h.astype(h_ref.dtype)
    if with_router:
        logits = jnp.dot(h, r_ref[...], preferred_element_type=_F32, precision=lax.Precision.HIGHEST)
        lane = lax.broadcasted_iota(jnp.int32, logits.shape, 1)
        valid = lane < N_EXPERTS
        lg = jnp.where(valid, logits, -jnp.inf)
        v1 = jnp.max(lg, axis=-1, keepdims=True)
        i1 = jnp.min(jnp.where(lg == v1, lane, V7X_LANES), axis=-1, keepdims=True)
        lg2 = jnp.where(lane == i1, -jnp.inf, lg)
        v2 = jnp.max(lg2, axis=-1, keepdims=True)
        i2 = jnp.min(jnp.where(lg2 == v2, lane, V7X_LANES), axis=-1, keepdims=True)
        e2 = jnp.exp(v2 - v1)
        p1 = 1.0 / (1.0 + e2)
        p2 = e2 / (1.0 + e2)
        comb_ref[...] = (jnp.where(lane == N_EXPERTS, i1.astype(_F32), 0.0)
                         + jnp.where(lane == N_EXPERTS + 1, i2.astype(_F32), 0.0)
                         + jnp.where(lane == N_EXPERTS + 2, p1, 0.0) + jnp.where(lane == N_EXPERTS + 3, p2, 0.0))


def _modnorm(x, norm_w, shift, scale, router=None, bl=256):
    b, L, d = x.shape
    bl = min(bl, L)
    assert L % bl == 0
    nl = L // bl
    per_batch = shift.shape[0] == b and b > 1
    mod_map = (lambda i, j: (i, 0, 0)) if per_batch else (lambda i, j: (0, 0, 0))
    in_specs = [pl.BlockSpec((1, bl, d), lambda i, j: (i, j, 0)),
                pl.BlockSpec((1, d), lambda i, j: (0, 0)),
                pl.BlockSpec((1, 1, d), mod_map),
                pl.BlockSpec((1, 1, d), mod_map)]
    args = [x, norm_w.reshape(1, d), shift, scale]
    out_specs = [pl.BlockSpec((bl, d), lambda i, j: (i * nl + j, 0))]
    with_router = router is not None
    out_shape = [jax.ShapeDtypeStruct((b * L, d), _F32 if with_router else _BF16)]
    if with_router:
        in_specs.append(pl.BlockSpec((d, V7X_LANES), lambda i, j: (0, 0)))
        args.append(router)
        out_specs.append(pl.BlockSpec((bl, V7X_LANES), lambda i, j: (i * nl + j, 0)))
        out_shape.append(jax.ShapeDtypeStruct((b * L, V7X_LANES), _F32))
    out = pl.pallas_call(
        functools.partial(_modnorm_kernel, with_router=with_router),
        grid=(b, nl),
        in_specs=in_specs,
        out_specs=out_specs,
        out_shape=out_shape,
        compiler_params=pltpu.CompilerParams(
            dimension_semantics=("parallel", "parallel"),
            vmem_limit_bytes=_vmem_limit(2 * bl * d * 4, 2 * bl * d * 2, 4 * bl * d * 4, 2 * d * V7X_LANES * 4)),
        name="modnorm",
    )(*args)
    return out if with_router else out[0]


def _final_norm_kernel(x_ref, w_ref, o_ref):
    x = x_ref[...]
    o_ref[...] = x * lax.rsqrt(jnp.mean(x * x, axis=-1, keepdims=True) + EPS) * w_ref[...]


def _final_norm(x, w, bl=256):
    m, d = x.shape
    return pl.pallas_call(
        _final_norm_kernel,
        grid=(m // bl,),
        in_specs=[pl.BlockSpec((bl, d), lambda i: (i, 0)), pl.BlockSpec((1, d), lambda i: (0, 0))],
        out_specs=pl.BlockSpec((bl, d), lambda i: (i, 0)),
        out_shape=jax.ShapeDtypeStruct((m, d), _F32),
        compiler_params=pltpu.CompilerParams(dimension_semantics=("parallel",),
                                             vmem_limit_bytes=_vmem_limit(6 * bl * d * 4)),
        name="final_norm",
    )(x, w.reshape(1, d))


_O_HF, _O_HB, _O_HGI, _O_MLK, _O_MLV = 0, 1024, 2048, 3072, 5120
_O_HGQ, _O_HGOG, _O_MLQ, _O_MLOG, _O_HY, _O_MERGE = 7168, 8192, 9216, 11264, 13312, 16384

HG_DK = HG_WIDTH // HG_HEADS
HG_CHUNK = 64
HG_SUB = 16
HG_BLOCK = 256
ML_CHUNK = 256


def _log_sigmoid(x):
    return jnp.minimum(x, 0.0) - jnp.log1p(jnp.exp(-jnp.abs(x)))


def _silu(x):
    return x * jax.nn.sigmoid(x)


def _logaddexp(a, b):
    return jnp.maximum(a, b) + jnp.log1p(jnp.exp(-jnp.abs(a - b)))


def _hgrn2_kernel(*refs, rev, with_out, finalize, has_init, n_chunks):
    refs = list(refs)
    pre_ref, v_ref = refs.pop(0), refs.pop(0)
    q_ref = refs.pop(0) if with_out else None
    lbc_ref = refs.pop(0)
    s0_ref = refs.pop(0) if has_init else None
    if finalize:
        ofwd_ref, og_ref, nw_ref = refs.pop(0), refs.pop(0), refs.pop(0)
    o_ref = refs.pop(0) if with_out else None
    st_ref = refs.pop(0)
    C, SUB = HG_CHUNK, HG_SUB
    n_sub = C // SUB

    @pl.when(pl.program_id(2) == 0)
    def _():
        st_ref[0, 0] = s0_ref[0, 0] if has_init else jnp.zeros(st_ref.shape[2:], _F32)

    lbc = lbc_ref[...]
    log_lb, log_1mlb, one_m_lb = lbc[0:1], lbc[1:2], lbc[2:3]
    r_i = lax.broadcasted_iota(jnp.int32, (C, C), 0)
    c_i = lax.broadcasted_iota(jnp.int32, (C, C), 1)
    tri = ((c_i >= r_i) if rev else (c_i <= r_i)).astype(_F32)
    row64 = lax.broadcasted_iota(jnp.int32, (C, 1), 0)
    row16 = lax.broadcasted_iota(jnp.int32, (SUB, 1), 0)
    lane64 = lax.broadcasted_iota(jnp.int32, (SUB, C), 1)
    last = 0 if rev else C - 1

    def chunk_terms(c):
        rows = slice(c * C, (c + 1) * C)
        pre = pre_ref[0, rows, :].astype(_F32)
        v = v_ref[0, rows, :]
        g = _logaddexp(log_lb, log_1mlb + _log_sigmoid(pre))
        kk = one_m_lb * jax.nn.sigmoid(-pre)
        b = jnp.dot(tri, g, preferred_element_type=_F32, precision=lax.Precision.HIGHEST)
        b_end = b[last:last + 1]
        kt = (kk * jnp.exp(b_end - b)).astype(_BF16)
        upd = lax.dot_general(v, kt, (((0,), (0,)), ((), ())), preferred_element_type=_F32)
        if not with_out:
            return jnp.exp(b_end), upd, None, None
        q = _silu(q_ref[0, rows, :].astype(_F32))
        qe = (q * jnp.exp(b)).astype(_BF16)
        bx = b - g
        atts = []
        for i in range(n_sub):
            sub = slice(i * SUB, (i + 1) * SUB)
            first = i * SUB + (SUB - 1 if rev else 0)
            piv = bx[first:first + 1]
            qi = q[sub] * jnp.exp(b[sub] - piv)
            earlier = (row64 >= (i + 1) * SUB) if rev else (row64 < i * SUB)
            khat = jnp.where(earlier, kk * jnp.exp(jnp.where(earlier, piv - b, 0.0)), 0.0)
            att = lax.dot_general(qi.astype(_BF16), khat.astype(_BF16),
                                  (((1,), (1,)), ((), ())), preferred_element_type=_F32)
            for s in range(SUB):
                r = i * SUB + s
                valid = (row16 <= s) if rev else (row16 >= s)
                rel = jnp.where(valid, b[sub] - b[r:r + 1], 0.0)
                z = jnp.where(valid, q[sub] * jnp.exp(rel) * kk[r:r + 1], 0.0)
                att = jnp.where(lane64 == r, jnp.sum(z, axis=1, keepdims=True), att)
            atts.append(att)
        att = jnp.concatenate(atts, axis=0).astype(_BF16)
        return jnp.exp(b_end), upd, qe, jnp.dot(att, v, preferred_element_type=_F32)

    terms = [chunk_terms(c) for c in range(n_chunks)]
    st = st_ref[0, 0]
    outs = [None] * n_chunks
    for c in (reversed(range(n_chunks)) if rev else range(n_chunks)):
        dec, upd, qe, o_intra = terms[c]
        if with_out:
            outs[c] = o_intra + lax.dot_general(qe, st.astype(_BF16), (((1,), (1,)), ((), ())),
                                                preferred_element_type=_F32)
        st = st * dec + upd
    st_ref[0, 0] = st
    if with_out:
        o = jnp.concatenate(outs, axis=0)
        if finalize:
            y = ofwd_ref[0] + o
            y = y * lax.rsqrt(jnp.mean(y * y, axis=-1, keepdims=True) + EPS) * nw_ref[...]
            o_ref[0] = (y * _silu(og_ref[0].astype(_F32))).astype(o_ref.dtype)
        else:
            o_ref[0] = o


def _hgrn2(p, lbc, *, d, with_out, init=None, o_fwd=None, norm_w=None):
    bsz, L, _ = p.shape
    rev = d == 1
    tb = min(L, HG_BLOCK)
    assert L % tb == 0 and tb % HG_CHUNK == 0
    nblk = L // tb
    finalize = o_fwd is not None
    has_init = init is not None
    blk = (lambda n: nblk - 1 - n) if rev else (lambda n: n)
    pre_off = (_O_HB if rev else _O_HF) // HG_DK

    def tok_spec(off):
        return pl.BlockSpec((1, tb, HG_DK), lambda b, h, n: (b, blk(n), off + h))

    in_specs = [tok_spec(pre_off), tok_spec(_O_HGI // HG_DK)]
    args = [p, p]
    if with_out:
        in_specs.append(tok_spec(_O_HGQ // HG_DK))
        args.append(p)
    in_specs.append(pl.BlockSpec((None, 8, HG_DK), lambda b, h, n: (d, 0, h)))
    args.append(lbc)
    state_spec = pl.BlockSpec((1, 1, HG_DK, HG_DK), lambda b, h, n: (b, h, 0, 0))
    if has_init:
        in_specs.append(state_spec)
        args.append(init)
    if finalize:
        in_specs += [tok_spec(0), tok_spec(_O_HGOG // HG_DK), pl.BlockSpec((1, HG_DK), lambda b, h, n: (0, h))]
        args += [o_fwd, p, norm_w.reshape(1, HG_WIDTH)]
    out_specs, out_shape = [], []
    if with_out:
        out_specs.append(tok_spec(0))
        out_shape.append(jax.ShapeDtypeStruct((bsz, L, HG_WIDTH), _BF16 if finalize else _F32))
    out_specs.append(state_spec)
    out_shape.append(jax.ShapeDtypeStruct((bsz, HG_HEADS, HG_DK, HG_DK), _F32))
    out = pl.pallas_call(
        functools.partial(_hgrn2_kernel, rev=rev, with_out=with_out, finalize=finalize, has_init=has_init,
                          n_chunks=tb // HG_CHUNK),
        grid=(bsz, HG_HEADS, nblk),
        in_specs=in_specs, out_specs=out_specs, out_shape=out_shape,
        compiler_params=pltpu.CompilerParams(dimension_semantics=("parallel", "parallel", "arbitrary")),
        name="hgrn2",
    )(*args)
    return (out[0], out[1]) if with_out else (None, out[0])


def _mlstm_kernel(*refs, rev, with_out, finalize, has_init, seg):
    refs = list(refs)
    k_ref, v_ref = refs.pop(0), refs.pop(0)
    q_ref = refs.pop(0) if with_out else None
    g_ref, ckw_ref, ckb_ref = refs.pop(0), refs.pop(0), refs.pop(0)
    if with_out:
        cqw_ref, cqb_ref = refs.pop(0), refs.pop(0)
    if has_init:
        c0_ref, n0_ref, m0_ref = refs.pop(0), refs.pop(0), refs.pop(0)
    if finalize:
        ofwd_ref, og_ref, nw_ref = refs.pop(0), refs.pop(0), refs.pop(0)
    o_ref = refs.pop(0) if with_out else None
    c_ref, n_ref, m_ref = refs
    C = k_ref.shape[1]

    @pl.when(pl.program_id(2) == 0)
    def _():
        if has_init:
            c_ref[...] = c0_ref[...]
            n_ref[...] = n0_ref[...]
            m_ref[...] = m0_ref[...]
        else:
            c_ref[...] = jnp.zeros(c_ref.shape, _F32)
            n_ref[...] = jnp.zeros(n_ref.shape, _F32)
            m_ref[...] = jnp.zeros(m_ref.shape, _F32)

    g = g_ref[0, 0]
    ji = 2 if rev else 0
    i_row = g[ji:ji + 1]
    f_row = _log_sigmoid(g[ji + 1:ji + 2])
    r_i = lax.broadcasted_iota(jnp.int32, (C, C), 0)
    c_i = lax.broadcasted_iota(jnp.int32, (C, C), 1)
    mask = (c_i >= r_i) if rev else (c_i <= r_i)
    upto = ((r_i >= c_i) if rev else (r_i <= c_i)).astype(_F32)
    b_row = jnp.dot(jnp.broadcast_to(f_row, (8, C)), upto, preferred_element_type=_F32,
                    precision=lax.Precision.HIGHEST)[0:1]
    b_col = jnp.sum(jnp.where(mask, f_row, 0.0), axis=1, keepdims=True)
    i_col = jnp.sum(jnp.where(r_i == c_i, i_row, 0.0), axis=1, keepdims=True)
    b_end = jnp.sum(f_row, axis=1, keepdims=True)
    m_prev = m_ref[0, 0][0:1, 0:1]
    cm = c_ref[0, 0]
    nrow = n_ref[0, 0]

    pos = lax.broadcasted_iota(jnp.int32, (C, 1), 0) % seg

    def conv(x_ref, w_ref, b_ref):
        x = x_ref[0].astype(_F32)
        w = w_ref[...]
        x_prev = jnp.where(pos == 0, 0.0, pltpu.roll(x, 1, 0))
        x_next = jnp.where(pos == seg - 1, 0.0, pltpu.roll(x, C - 1, 0))
        return b_ref[...] + w[0:1] * x_prev + w[1:2] * x + w[2:3] * x_next

    k = _silu(conv(k_ref, ckw_ref, ckb_ref)) * ML_DH ** -0.5
    v = v_ref[0]

    if with_out:
        q = _silu(conv(q_ref, cqw_ref, cqb_ref))
        qb = q.astype(_BF16)
        logd = b_col - b_row + i_row
        m_t = jnp.maximum(b_col + m_prev, jnp.max(jnp.where(mask, logd, NEG_BIG), axis=1, keepdims=True))
        prev = jnp.exp(b_col + m_prev - m_t)
        dmat = jnp.where(mask, jnp.exp(jnp.where(mask, logd - m_t, 0.0)), 0.0)
        s = lax.dot_general(qb, k.astype(_BF16), (((1,), (1,)), ((), ())), preferred_element_type=_F32) * dmat
        num = prev * jnp.dot(qb, cm.astype(_BF16), preferred_element_type=_F32) \
            + jnp.dot(s.astype(_BF16), v, preferred_element_type=_F32)
        den = prev * jnp.sum(q * nrow, axis=1, keepdims=True) + jnp.sum(s, axis=1, keepdims=True)
        h = num / jnp.maximum(jnp.abs(den), jnp.exp(-m_t))
        if finalize:
            y = ofwd_ref[0] + h
            y = y * lax.rsqrt(jnp.mean(y * y, axis=-1, keepdims=True) + EPS) * nw_ref[...]
            o_ref[0] = (y * jax.nn.sigmoid(og_ref[0].astype(_F32))).astype(o_ref.dtype)
        else:
            o_ref[0] = h

    log_w = b_end - b_col + i_col
    m_new = jnp.maximum(b_end + m_prev, jnp.max(log_w, axis=0, keepdims=True))
    decay = jnp.exp(b_end + m_prev - m_new)
    wk = jnp.exp(log_w - m_new) * k
    c_ref[0, 0] = decay * cm + lax.dot_general(wk.astype(_BF16), v, (((0,), (0,)), ((), ())),
                                               preferred_element_type=_F32)
    n_ref[0, 0] = decay * nrow + jnp.sum(wk, axis=0, keepdims=True)
    m_ref[0, 0] = jnp.broadcast_to(m_new, m_ref.shape[2:])


def _mlstm(p, gate_rows, conv_w, conv_b, layer, *, d, seg, with_out, init=None, o_fwd=None, norm_w=None):
    bsz, L, _ = p.shape
    rev = d == 1
    C = min(L, ML_CHUNK)
    assert L % C == 0 and C % seg == 0
    nch = L // C
    finalize = o_fwd is not None
    has_init = init is not None
    blk = (lambda n: nch - 1 - n) if rev else (lambda n: n)

    def tok_spec(off):
        return pl.BlockSpec((1, C, ML_DH), lambda b, h, n: (b, blk(n), off + h))

    def conv_specs(off):
        return [pl.BlockSpec((None, SHORT_CONV, ML_DH), lambda b, h, n: (layer, 0, off + h)),
                pl.BlockSpec((None, 1, ML_DH), lambda b, h, n: (layer, 0, off + h))]

    conv_b3 = conv_b.reshape(DEPTH, 1, 2 * ML_WIDTH)
    in_specs = [tok_spec(_O_MLK // ML_DH), tok_spec(_O_MLV // ML_DH)]
    args = [p, p]
    if with_out:
        in_specs.append(tok_spec(_O_MLQ // ML_DH))
        args.append(p)
    in_specs.append(pl.BlockSpec((1, 1, 8, C), lambda b, h, n: (b, h, 0, blk(n))))
    args.append(gate_rows)
    in_specs += conv_specs(0)
    args += [conv_w, conv_b3]
    if with_out:
        in_specs += conv_specs(ML_HEADS)
        args += [conv_w, conv_b3]
    state_specs = [pl.BlockSpec((1, 1, ML_DH, ML_DH), lambda b, h, n: (b, h, 0, 0)),
                   pl.BlockSpec((1, 1, 1, ML_DH), lambda b, h, n: (b, h, 0, 0)),
                   pl.BlockSpec((1, 1, 8, V7X_LANES), lambda b, h, n: (b, h, 0, 0))]
    state_shape = [jax.ShapeDtypeStruct((bsz, ML_HEADS, ML_DH, ML_DH), _F32),
                   jax.ShapeDtypeStruct((bsz, ML_HEADS, 1, ML_DH), _F32),
                   jax.ShapeDtypeStruct((bsz, ML_HEADS, 8, V7X_LANES), _F32)]
    if has_init:
        in_specs += state_specs
        args += list(init)
    if finalize:
        in_specs += [tok_spec(0), tok_spec(_O_MLOG // ML_DH), pl.BlockSpec((1, ML_DH), lambda b, h, n: (0, h))]
        args += [o_fwd, p, norm_w.reshape(1, ML_WIDTH)]
    out_specs, out_shape = [], []
    if with_out:
        out_specs.append(tok_spec(0))
        out_shape.append(jax.ShapeDtypeStruct((bsz, L, ML_WIDTH), _BF16 if finalize else _F32))
    out = pl.pallas_call(
        functools.partial(_mlstm_kernel, rev=rev, with_out=with_out, finalize=finalize, has_init=has_init, seg=seg),
        grid=(bsz, ML_HEADS, nch),
        in_specs=in_specs, out_specs=out_specs + state_specs, out_shape=out_shape + state_shape,
        compiler_params=pltpu.CompilerParams(dimension_semantics=("parallel", "parallel", "arbitrary")),
        name="mlstm",
    )(*args)
    return (out[0], tuple(out[1:])) if with_out else (None, tuple(out))


def _dft_kernel(c_ref, s_ref, st_ref, *, n_fft, bm):
    shape = c_ref.shape
    row = lax.broadcasted_iota(jnp.int32, shape, 0) + pl.program_id(0) * bm
    col = lax.broadcasted_iota(jnp.int32, shape, 1)
    ang = ((row * col) & (n_fft - 1)).astype(_F32) * (2.0 * math.pi / n_fft)
    alt_col = jnp.where((col & 1) == 0, 1.0, -1.0)
    alt_row = jnp.where((row & 1) == 0, 1.0, -1.0)
    msin = -jnp.sin(ang)
    c_ref[...] = jnp.cos(ang).astype(c_ref.dtype)
    s_ref[...] = jnp.where(row == 0, alt_col, msin).astype(s_ref.dtype)
    st_ref[...] = jnp.where(col == 0, alt_row, msin).astype(st_ref.dtype)


def _dft_mats(L, bm=256):
    bm = min(bm, L)
    spec = pl.BlockSpec((bm, L), lambda i: (i, 0))
    shape = jax.ShapeDtypeStruct((L, L), _BF16)
    return pl.pallas_call(
        functools.partial(_dft_kernel, n_fft=2 * L, bm=bm),
        grid=(L // bm,), in_specs=[], out_specs=[spec] * 3, out_shape=[shape] * 3,
        compiler_params=pltpu.CompilerParams(dimension_semantics=("parallel",)),
        name="dft_mats",
    )()


def _hy_filter_kernel(band_ref, w1t_ref, w1c_ref, w1s_ref, b1_ref, w2_ref, b2_ref, w3_ref, fr_ref, dl_ref,
                      klo_ref, khi_ref, asum_ref, *, L, bl):
    hp = lax.Precision.HIGHEST
    m = (lax.broadcasted_iota(jnp.int32, (bl, 1), 0) + pl.program_id(0) * bl)
    half = HY_ORDER * HY_WIDTH
    total = jnp.zeros((1, half), _F32)
    for side, out_ref in ((0, klo_ref), (1, khi_ref)):
        pos = (m if side == 0 else (L - 1 - m)).astype(_F32)
        t = pos / (L - 1)
        ang = (2.0 * math.pi / L) * pos * band_ref[...]
        a = t * w1t_ref[...] + jnp.dot(jnp.cos(ang), w1c_ref[...], precision=hp, preferred_element_type=_F32) \
            + jnp.dot(jnp.sin(ang), w1s_ref[...], precision=hp, preferred_element_type=_F32) + b1_ref[...]
        a = jnp.sin(fr_ref[0:1] * a)
        a = jnp.sin(fr_ref[1:2] * (jnp.dot(a, w2_ref[...], precision=hp, preferred_element_type=_F32) + b2_ref[...]))
        f = jnp.dot(a, w3_ref[:, side * half:(side + 1) * half], precision=hp, preferred_element_type=_F32)
        f = f * jnp.exp(-t * dl_ref[...])
        if side == 1:
            f = jnp.where(m == 0, 0.0, f)
        out_ref[...] = f.astype(out_ref.dtype)
        total = total + jnp.sum(jnp.abs(f), axis=0, keepdims=True)

    @pl.when(pl.program_id(0) == 0)
    def _():
        asum_ref[...] = jnp.zeros(asum_ref.shape, _F32)

    asum_ref[...] += jnp.broadcast_to(total, asum_ref.shape)


def _hy_filter(L, w1, b1, w2, b2, w3, freq, bl=256):
    bl = min(bl, L)
    ffn = w1.shape[1]
    half = HY_ORDER * HY_WIDTH
    bands = jnp.pad(jnp.linspace(1e-4, HY_BANDS - 1, HY_BANDS, dtype=_F32), (0, V7X_LANES - HY_BANDS)).reshape(1, -1)
    pad_rows = ((0, V7X_LANES - HY_BANDS), (0, 0))
    deltas = jnp.abs(jnp.linspace(HY_MIN_DECAY, HY_MAX_DECAY, HY_WIDTH, dtype=_F32))
    args = [bands, w1[0:1], jnp.pad(w1[1:1 + HY_BANDS], pad_rows), jnp.pad(w1[1 + HY_BANDS:], pad_rows),
            b1.reshape(1, ffn), w2, b2.reshape(1, ffn), w3, freq, jnp.tile(deltas, HY_ORDER).reshape(1, half)]
    full = lambda a: pl.BlockSpec(a.shape, lambda i: (0,) * a.ndim)
    return pl.pallas_call(
        functools.partial(_hy_filter_kernel, L=L, bl=bl),
        grid=(L // bl,),
        in_specs=[full(a) for a in args],
        out_specs=[pl.BlockSpec((bl, half), lambda i: (i, 0)), pl.BlockSpec((bl, half), lambda i: (i, 0)),
                   pl.BlockSpec((8, half), lambda i: (0, 0))],
        out_shape=[jax.ShapeDtypeStruct((L, half), _BF16), jax.ShapeDtypeStruct((L, half), _BF16),
                   jax.ShapeDtypeStruct((8, half), _F32)],
        compiler_params=pltpu.CompilerParams(dimension_semantics=("arbitrary",)),
        name="hy_filter",
    )(*args)


def _hy_kf_kernel(t_ref, klo_ref, khi_ref, asum_ref, o_ref, *, bm):
    row = lax.broadcasted_iota(jnp.int32, (bm, 1), 0) + pl.program_id(0) * bm
    sgn = jnp.where((row & 1) == 0, 1.0, -1.0)
    t = t_ref[...]
    acc = jnp.dot(t, klo_ref[...], preferred_element_type=_F32) \
        + sgn * jnp.dot(t, khi_ref[...], preferred_element_type=_F32)
    o_ref[...] = acc / (asum_ref[0:1] + EPS)


def _hy_kf(cmat, smat, klo, khi, asum, bm=512, bn=512):
    L = cmat.shape[0]
    tmat = jnp.concatenate([cmat, smat], axis=0)
    half = klo.shape[1]
    bm, bn = min(bm, L), min(bn, half)
    return pl.pallas_call(
        functools.partial(_hy_kf_kernel, bm=bm),
        grid=(2 * L // bm, half // bn),
        in_specs=[pl.BlockSpec((bm, L), lambda i, j: (i, 0)),
                  pl.BlockSpec((L, bn), lambda i, j: (0, j)), pl.BlockSpec((L, bn), lambda i, j: (0, j)),
                  pl.BlockSpec((8, bn), lambda i, j: (0, j))],
        out_specs=pl.BlockSpec((bm, bn), lambda i, j: (i, j)),
        out_shape=jax.ShapeDtypeStruct((2 * L, half), _F32),
        compiler_params=pltpu.CompilerParams(
            dimension_semantics=("parallel", "parallel"),
            vmem_limit_bytes=_vmem_limit(2 * bm * L * 2, 4 * L * bn * 2, 6 * bm * bn * 4)),
        name="hy_kf",
    )(tmat, klo, khi, asum)


def _hy_split_kernel(*refs, seg):
    p_refs, w_refs, b_refs, o_refs = refs[0:3], refs[3:6], refs[6:9], refs[9:12]
    bl = o_refs[0].shape[1]
    pos = lax.broadcasted_iota(jnp.int32, (bl, 1), 0) % seg
    for p_ref, w_ref, b_ref, o_ref in zip(p_refs, w_refs, b_refs, o_refs):
        x = p_ref[0].astype(_F32)
        w = w_ref[...]
        x_prev = jnp.where(pos == 0, 0.0, pltpu.roll(x, 1, 0))
        x_next = jnp.where(pos == seg - 1, 0.0, pltpu.roll(x, bl - 1, 0))
        o_ref[0] = (b_ref[...] + w[0:1] * x_prev + w[1:2] * x + w[2:3] * x_next).astype(o_ref.dtype)


def _hy_split(p, conv_w, conv_b, seg):
    bsz, L, _ = p.shape
    bl = min(L, 256)
    assert L % bl == 0 and bl % seg == 0
    base = _O_HY // HY_WIDTH
    out_spec = pl.BlockSpec((1, bl, HY_WIDTH), lambda b, i: (b, i, 0))
    conv_b2 = conv_b.reshape(1, 3 * HY_WIDTH)
    in_specs = [pl.BlockSpec((1, bl, HY_WIDTH), lambda b, i, j=j: (b, i, base + j)) for j in range(3)]
    in_specs += [pl.BlockSpec((SHORT_CONV, HY_WIDTH), lambda b, i, j=j: (0, j)) for j in range(3)]
    in_specs += [pl.BlockSpec((1, HY_WIDTH), lambda b, i, j=j: (0, j)) for j in range(3)]
    return pl.pallas_call(
        functools.partial(_hy_split_kernel, seg=seg),
        grid=(bsz, L // bl),
        in_specs=in_specs,
        out_specs=[out_spec] * 3,
        out_shape=[jax.ShapeDtypeStruct((bsz, L, HY_WIDTH), _BF16)] * 3,
        compiler_params=pltpu.CompilerParams(dimension_semantics=("parallel", "parallel")),
        name="hy_split",
    )(p, p, p, conv_w, conv_w, conv_w, conv_b2, conv_b2, conv_b2)


def _hy_fwd_kernel(c_ref, s_ref, z_ref, kre_ref, kim_ref, pre_ref, pim_ref, *, bm):
    z = z_ref[...]
    xre = jnp.dot(c_ref[...], z, preferred_element_type=_F32)
    xim = jnp.dot(s_ref[...], z, preferred_element_type=_F32)
    kre, kim = kre_ref[...], kim_ref[...]
    row0 = (lax.broadcasted_iota(jnp.int32, (bm, 1), 0) + pl.program_id(0) * bm) == 0
    pre = jnp.where(row0, 0.5 * xre * kre, xre * kre - xim * kim)
    pim = jnp.where(row0, 0.5 * xim * kim, xre * kim + xim * kre)
    pre_ref[...] = pre.astype(pre_ref.dtype)
    pim_ref[...] = pim.astype(pim_ref.dtype)


def _hy_fwd(cmat, smat, z, kf, order, bm=512, bn=512):
    bsz, L, width = z.shape
    bm, bn = min(bm, L), min(bn, width)
    nj = width // bn
    mat_spec = pl.BlockSpec((bm, L), lambda i, b, j: (i, 0))
    out_spec = pl.BlockSpec((None, bm, bn), lambda i, b, j: (b, i, j))
    out_shape = jax.ShapeDtypeStruct((bsz, L, width), _BF16)
    return pl.pallas_call(
        functools.partial(_hy_fwd_kernel, bm=bm),
        grid=(L // bm, bsz, nj),
        in_specs=[mat_spec, mat_spec,
                  pl.BlockSpec((None, L, bn), lambda i, b, j: (b, 0, j)),
                  pl.BlockSpec((bm, bn), lambda i, b, j: (i, order * nj + j)),
                  pl.BlockSpec((bm, bn), lambda i, b, j: (L // bm + i, order * nj + j))],
        out_specs=[out_spec, out_spec], out_shape=[out_shape, out_shape],
        compiler_params=pltpu.CompilerParams(
            dimension_semantics=("parallel", "parallel", "parallel"),
            vmem_limit_bytes=_vmem_limit(4 * bm * L * 2, 2 * L * bn * 2, 4 * bm * bn * 4, 4 * bm * bn * 2,
                                         4 * bm * bn * 4)),
        name="hy_fwd",
    )(cmat, smat, z, kf, kf)


def _hy_inv_kernel(c_ref, st_ref, pre_ref, pim_ref, z_ref, g_ref, skip_ref, o_ref, *, scale):
    conv = jnp.dot(c_ref[...], pre_ref[...], preferred_element_type=_F32) \
        + jnp.dot(st_ref[...], pim_ref[...], preferred_element_type=_F32)
    z = z_ref[...].astype(_F32)
    o_ref[...] = (g_ref[...].astype(_F32) * (scale * conv + skip_ref[...] * z)).astype(o_ref.dtype)


def _hy_inv(cmat, smat_t, pre, pim, z, gate, skip, bm=512, bn=512):
    bsz, L, width = z.shape
    bm, bn = min(bm, L), min(bn, width)
    mat_spec = pl.BlockSpec((bm, L), lambda i, b, j: (i, 0))
    spec_spec = pl.BlockSpec((None, L, bn), lambda i, b, j: (b, 0, j))
    tok_spec = pl.BlockSpec((None, bm, bn), lambda i, b, j: (b, i, j))
    return pl.pallas_call(
        functools.partial(_hy_inv_kernel, scale=1.0 / L),
        grid=(L // bm, bsz, width // bn),
        in_specs=[mat_spec, mat_spec, spec_spec, spec_spec, tok_spec, tok_spec,
                  pl.BlockSpec((1, bn), lambda i, b, j: (0, j))],
        out_specs=tok_spec,
        out_shape=jax.ShapeDtypeStruct((bsz, L, width), _BF16),
        compiler_params=pltpu.CompilerParams(
            dimension_semantics=("parallel", "parallel", "parallel"),
            vmem_limit_bytes=_vmem_limit(4 * bm * L * 2, 4 * L * bn * 2, 6 * bm * bn * 2, 3 * bm * bn * 4)),
        name="hy_inv",
    )(cmat, smat_t, pre, pim, z, gate, skip.reshape(1, width))


def _hyena(p, lp, seg, dft):
    L = p.shape[1]
    cmat, smat, smat_t = dft
    klo, khi, asum = _hy_filter(L, lp["hy_w1"], lp["hy_b1"], lp["hy_w2"], lp["hy_b2"], lp["hy_w3"], lp["hy_freq"])
    kf = _hy_kf(cmat, smat, klo, khi, asum)
    z, x1, x2 = _hy_split(p, lp["hy_conv_w"], lp["hy_conv_b"], seg)
    for o, gate in enumerate((x1, x2)):
        pre, pim = _hy_fwd(cmat, smat, z, kf, o)
        z = _hy_inv(cmat, smat_t, pre, pim, z, gate, lp["hy_skip"][o])
    return z


def _prep_layer_weights(l, w_in, w_branch, w_out):
    wl = w_in[l]
    w_main = jnp.concatenate([wl[:, :N_STATE_MAIN], wl[:, N_STATE_COLS:]], axis=1).astype(_BF16)
    w_gate = jnp.pad(wl[:, N_STATE_MAIN:N_STATE_COLS], ((0, 0), (0, V7X_LANES - N_GATE_COLS))).astype(_BF16)
    return w_main, w_gate, w_branch[l].astype(_BF16), w_out[l].astype(_BF16)


def _lower_bound_rows(lb):
    lb = lb.astype(_F32)
    rows = jnp.stack([jnp.log(jnp.maximum(lb, LB_FLOOR)), jnp.log1p(-lb), 1.0 - lb], axis=1)
    return jnp.pad(rows, ((0, 0), (0, 5), (0, 0)))


def _gate_rows(gates, gate_b):
    bsz, L, _ = gates.shape
    g = (gates[..., :N_GATE_COLS] + gate_b).reshape(bsz, L, 4, ML_HEADS)
    return jnp.pad(g.transpose(0, 3, 2, 1), ((0, 0), (0, 0), (0, 4), (0, 0)))


def _stream_mixer(xs, mods, lw, lp, layer, n_rows, init, with_out):
    w_main, w_gate, wb, wo = lw
    bsz, L, d = xs.shape
    sh, sc, g = mods[0], mods[1], mods[2]
    h = _modnorm(xs, lp["norm1"], sh, sc)
    w_proj = w_main if with_out else w_main[:, :N_STATE_MAIN]
    p = _matmul(h, w_proj, out_dtype=_BF16).reshape(bsz, L, -1)
    gates = _matmul(h, w_gate, out_dtype=_F32).reshape(bsz, L, V7X_LANES)
    grows = _gate_rows(gates, lp["ml_gate_b"])
    seg = L // n_rows
    hg_init, ml_init = init if init is not None else ((None, None), (None, None))
    hg_kw = dict(with_out=with_out)
    ml_kw = dict(seg=seg, with_out=with_out)
    o_hg, s_hg_f = _hgrn2(p, lp["lbc"], d=0, init=hg_init[0], **hg_kw)
    y_hg, s_hg_b = _hgrn2(p, lp["lbc"], d=1, init=hg_init[1], o_fwd=o_hg, norm_w=lp["hg_norm_w"], **hg_kw)
    o_ml, s_ml_f = _mlstm(p, grows, lp["ml_conv_w"], lp["ml_conv_b"], layer, d=0, init=ml_init[0], **ml_kw)
    y_ml, s_ml_b = _mlstm(p, grows, lp["ml_conv_w"], lp["ml_conv_b"], layer, d=1, init=ml_init[1],
                          o_fwd=o_ml, norm_w=lp["ml_norm_w"], **ml_kw)
    finals = ((s_hg_f, s_hg_b), (s_ml_f, s_ml_b))
    if not with_out:
        return None, finals
    y_hy = _hyena(p, lp, seg, lp["dft"][L])
    m = bsz * L
    merged = _merge(y_hg.reshape(m, -1), y_ml.reshape(m, -1), y_hy.reshape(m, -1), wb, p.reshape(m, -1), _O_MERGE)
    x_new = _matmul(merged, wo, out_dtype=_F32, res=xs.reshape(m, d), gate=g,
                    rows_per_gate=L if g.shape[0] > 1 else m)
    return x_new.reshape(bsz, L, d), finals


def _stream_ffn(xs, mods, norm2, l, ffn):
    bsz, L, d = xs.shape
    m = bsz * L
    sh2, sc2, g2 = mods[3], mods[4], mods[5]
    rows_per_gate = L if g2.shape[0] > 1 else m
    if l % 2 == 0:
        wg, wu, wd = ffn["dense"]
        h = _modnorm(xs, norm2, sh2, sc2)
        hid = _swiglu(h, wg, wu)
        out = _matmul(hid, wd, out_dtype=_F32, bk=DENSE_FF_PAD // 4, res=xs.reshape(m, d), gate=g2,
                      rows_per_gate=rows_per_gate)
    else:
        router, wg, wu, wd = ffn["moe"]
        h, route = _modnorm(xs, norm2, sh2, sc2, router=router)
        out = _moe(h, route, xs.reshape(m, d), g2, rows_per_gate, wg, wu, wd)
    return out.reshape(bsz, L, d)


def kernel(x, c, ctx, c_ctx, norm1_w, norm2_w, ada_w, ada_b, w_in, hg_lb_logits, hg_norm_w,
           ml_conv_w, ml_conv_b, ml_gate_b, ml_norm_w, hy_conv_w, hy_conv_b, hy_w1, hy_b1,
           hy_w2, hy_b2, hy_w3, hy_freq, hy_skip, w_branch, w_out, dense_w_gate, dense_w_up,
           dense_w_down, moe_router, moe_w_gate, moe_w_up, moe_w_down, final_norm_w):
    bsz, L, d = x.shape
    rows = L // GRID_W
    lb_w = jax.nn.softmax(hg_lb_logits.astype(_F32), axis=0)
    lower_bounds = jnp.cumsum(lb_w, axis=0) - lb_w[0]

    cond = jnp.zeros((8, d), _F32).at[:bsz].set(c).at[bsz].set(c_ctx)
    dft = {n: _dft_mats(n) for n in {L, ctx.shape[1]}}
    for l in range(DEPTH):
        lp = {
            "norm1": norm1_w[l], "hg_norm_w": hg_norm_w[l], "lbc": _lower_bound_rows(lower_bounds[l]), "dft": dft,
            "ml_conv_w": ml_conv_w, "ml_conv_b": ml_conv_b, "ml_gate_b": ml_gate_b[l],
            "ml_norm_w": ml_norm_w[l], "hy_conv_w": hy_conv_w[l], "hy_conv_b": hy_conv_b[l],
            "hy_w1": hy_w1[l], "hy_b1": hy_b1[l], "hy_w2": hy_w2[l], "hy_b2": hy_b2[l],
            "hy_w3": hy_w3[l], "hy_freq": hy_freq[l], "hy_skip": hy_skip[l],
        }
        lw = _prep_layer_weights(l, w_in, w_branch, w_out)
        if l % 2 == 0:
            i = l // 2
            pad = DENSE_FF_PAD - DENSE_FF
            ffn = {"dense": (jnp.pad(dense_w_gate[i].astype(_BF16), ((0, 0), (0, pad))),
                             jnp.pad(dense_w_up[i].astype(_BF16), ((0, 0), (0, pad))),
                             jnp.pad(dense_w_down[i].astype(_BF16), ((0, pad), (0, 0))))}
        else:
            i = l // 2
            ffn = {"moe": (jnp.pad(moe_router[i], ((0, 0), (0, V7X_LANES - N_EXPERTS))),
                           moe_w_gate[i].astype(_BF16), moe_w_up[i].astype(_BF16), moe_w_down[i].astype(_BF16))}
        mod = _ada(cond, ada_w, ada_b, l)
        mods_lat = [mod[:bsz, j * d:(j + 1) * d].reshape(bsz, 1, d) for j in range(N_MOD)]
        mods_ctx = [mod[bsz:bsz + 1, j * d:(j + 1) * d].reshape(1, 1, d) for j in range(N_MOD)]
        last = l == DEPTH - 1
        ctx_new, ctx_fin = _stream_mixer(ctx, mods_ctx, lw, lp, l, 1, None, not last)
        if not last:
            ctx = _stream_ffn(ctx_new, mods_ctx, norm2_w[l], l, ffn)
        x, _ = _stream_mixer(x, mods_lat, lw, lp, l, rows, ctx_fin, True)
        x = _stream_ffn(x, mods_lat, norm2_w[l], l, ffn)
    return _final_norm(x.reshape(bsz * L, d), final_norm_w).reshape(bsz, L, d)
```

```python
import functools
import math

import jax
import jax.numpy as jnp
from jax import lax
from jax.experimental import pallas as pl
from jax.experimental.pallas import tpu as pltpu

D_MODEL = 4096
DEPTH = 2
GRID_W = 64
EPS = 1e-6
SHORT_CONV = 3
N_MOD = 6
NEG_BIG = -1e30
LB_FLOOR = 1e-30

HG_WIDTH = D_MODEL // 4
HG_HEADS = 8
ML_WIDTH = D_MODEL // 2
ML_HEADS = 8
ML_DH = ML_WIDTH // ML_HEADS
HY_WIDTH = D_MODEL // 4
HY_ORDER = 2
HY_POS_DIM = 33
HY_BANDS = (HY_POS_DIM - 1) // 2
HY_MIN_DECAY = math.log(1e-2) / 1.5
HY_MAX_DECAY = math.log(1e-2) / 0.3
N_BRANCH = 3

N_GATE_COLS = 4 * ML_HEADS
N_STATE_MAIN = 3 * HG_WIDTH + 2 * ML_WIDTH
N_STATE_COLS = N_STATE_MAIN + N_GATE_COLS
N_MAIN = N_STATE_MAIN + 2 * HG_WIDTH + 2 * ML_WIDTH + 3 * HY_WIDTH + N_BRANCH * D_MODEL

DENSE_FF = 11008
N_EXPERTS = 8
TOP_K = 2
EXPERT_FF = 2048

V7X_LANES = 128
V7X_VMEM_BYTES = 64 * 1024 * 1024
DENSE_FF_PAD = 11264

_F32 = jnp.float32
_BF16 = jnp.bfloat16


def _vmem_limit(*block_bytes):
    need = int(sum(block_bytes)) + (4 << 20)
    return min(max(need, 16 << 20), V7X_VMEM_BYTES - (6 << 20))


def _mm_kernel(*refs, nk, has_res):
    if has_res:
        x_ref, w_ref, res_ref, gate_ref = refs[:4]
        rest = refs[4:]
    else:
        x_ref, w_ref = refs[:2]
        rest = refs[2:]
    o_ref = rest[0]

    def finish(acc):
        if has_res:
            acc = res_ref[...] + gate_ref[0] * acc
        o_ref[...] = acc.astype(o_ref.dtype)

    part = jnp.dot(x_ref[...], w_ref[...], preferred_element_type=_F32)
    if nk == 1:
        finish(part)
        return
    acc_ref = rest[1]
    k = pl.program_id(2)

    @pl.when(k == 0)
    def _():
        acc_ref[...] = part

    @pl.when(k > 0)
    def _():
        acc_ref[...] += part

    @pl.when(k == nk - 1)
    def _():
        finish(acc_ref[...])


def _matmul(x, w, *, out_dtype, bm=1024, bn=1024, bk=None, res=None, gate=None, rows_per_gate=None):
    m, kdim = x.shape
    n = w.shape[1]
    bm, bn = min(bm, m), min(bn, n)
    bk = kdim if bk is None else bk
    assert m % bm == 0 and n % bn == 0 and kdim % bk == 0, (x.shape, w.shape, bm, bn, bk)
    nk = kdim // bk
    has_res = res is not None
    in_specs = [pl.BlockSpec((bm, bk), lambda i, j, k: (i, k)),
                pl.BlockSpec((bk, bn), lambda i, j, k: (k, j))]
    args = [x, w]
    osz = jnp.dtype(out_dtype).itemsize
    vm = [2 * bm * bk * 2, 2 * bk * bn * 2, 2 * bm * bn * osz, 2 * bm * bn * 4]
    if has_res:
        blocks_per_gate = rows_per_gate // bm
        assert rows_per_gate % bm == 0
        in_specs += [pl.BlockSpec((bm, bn), lambda i, j, k: (i, j)),
                     pl.BlockSpec((1, 1, bn), lambda i, j, k: (i // blocks_per_gate, 0, j))]
        args += [res, gate]
        vm.append(2 * bm * bn * 4)
    scratch = [pltpu.VMEM((bm, bn), _F32)] if nk > 1 else []
    return pl.pallas_call(
        functools.partial(_mm_kernel, nk=nk, has_res=has_res),
        grid=(m // bm, n // bn, nk),
        in_specs=in_specs,
        out_specs=pl.BlockSpec((bm, bn), lambda i, j, k: (i, j)),
        out_shape=jax.ShapeDtypeStruct((m, n), out_dtype),
        scratch_shapes=scratch,
        compiler_params=pltpu.CompilerParams(
            dimension_semantics=("parallel", "parallel", "arbitrary"),
            vmem_limit_bytes=_vmem_limit(*vm)),
        name="matmul",
    )(*args)


def _swiglu_kernel(x_ref, wg_ref, wu_ref, o_ref):
    x = x_ref[...]
    g = jnp.dot(x, wg_ref[...], preferred_element_type=_F32)
    u = jnp.dot(x, wu_ref[...], preferred_element_type=_F32)
    o_ref[...] = (g * jax.nn.sigmoid(g) * u).astype(o_ref.dtype)


def _swiglu(x, wg, wu, *, bm=1024, bn=512):
    m, kdim = x.shape
    n = wg.shape[1]
    bm = min(bm, m)
    assert m % bm == 0 and n % bn == 0
    return pl.pallas_call(
        _swiglu_kernel,
        grid=(m // bm, n // bn),
        in_specs=[pl.BlockSpec((bm, kdim), lambda i, j: (i, 0)),
                  pl.BlockSpec((kdim, bn), lambda i, j: (0, j)),
                  pl.BlockSpec((kdim, bn), lambda i, j: (0, j))],
        out_specs=pl.BlockSpec((bm, bn), lambda i, j: (i, j)),
        out_shape=jax.ShapeDtypeStruct((m, n), _BF16),
        compiler_params=pltpu.CompilerParams(
            dimension_semantics=("parallel", "parallel"),
            vmem_limit_bytes=_vmem_limit(2 * bm * kdim * 2, 4 * kdim * bn * 2, 2 * bm * bn * 2, 4 * bm * bn * 4)),
        name="swiglu",
    )(x, wg, wu)


MOE_TILE = 512


def _route_plan(route, bm):
    m = route.shape[0]
    e_flat = route[:, N_EXPERTS:N_EXPERTS + TOP_K].astype(jnp.int32).T.reshape(-1)
    p_flat = route[:, N_EXPERTS + TOP_K:N_EXPERTS + 2 * TOP_K].T.reshape(-1)
    onehot = (e_flat[:, None] == jnp.arange(N_EXPERTS)[None, :]).astype(jnp.int32)
    csum = jnp.cumsum(onehot, axis=0)
    rank = jnp.sum((csum - onehot) * onehot, axis=1)
    padded = (csum[-1] + bm - 1) // bm * bm
    ends = jnp.cumsum(padded)
    dest = ((ends - padded)[e_flat] + rank).astype(jnp.int32)
    n_rows = TOP_K * m + N_EXPERTS * bm
    src = jnp.zeros((n_rows,), jnp.int32).at[dest].set(jnp.tile(jnp.arange(m, dtype=jnp.int32), TOP_K))
    wrow = jnp.zeros((n_rows,), _F32).at[dest].set(p_flat)
    tile_start = jnp.arange(n_rows // bm, dtype=jnp.int32) * bm
    tile_e = jnp.minimum(jnp.sum(tile_start[:, None] >= ends[None, :], axis=1), N_EXPERTS - 1).astype(jnp.int32)
    n_used = (ends[-1] // bm).astype(jnp.int32).reshape(1)
    return src, wrow, tile_e, n_used, dest


def _gather_kernel(idx_ref, x_hbm, o_ref, sem, *, bm):
    base = pl.program_id(0) * bm

    def row_copy(r, src_row):
        return pltpu.make_async_copy(x_hbm.at[pl.ds(src_row, 1)], o_ref.at[pl.ds(r, 1)], sem)

    def issue(r, c):
        row_copy(r, idx_ref[base + r]).start()
        return c

    def drain(r, c):
        row_copy(r, 0).wait()
        return c

    lax.fori_loop(0, bm, issue, 0, unroll=8)
    lax.fori_loop(0, bm, drain, 0, unroll=8)


def _gather_rows(x, idx, bm=256):
    n = idx.shape[0]
    d = x.shape[1]
    assert n % bm == 0
    return pl.pallas_call(
        functools.partial(_gather_kernel, bm=bm),
        grid_spec=pltpu.PrefetchScalarGridSpec(
            num_scalar_prefetch=1, grid=(n // bm,),
            in_specs=[pl.BlockSpec(memory_space=pl.ANY)],
            out_specs=pl.BlockSpec((bm, d), lambda i, idx_ref: (i, 0)),
            scratch_shapes=[pltpu.SemaphoreType.DMA(())]),
        out_shape=jax.ShapeDtypeStruct((n, d), x.dtype),
        compiler_params=pltpu.CompilerParams(dimension_semantics=("arbitrary",)),
        name="gather_rows",
    )(idx, x)


def _gswiglu_kernel(te_ref, nu_ref, x_ref, wg_ref, wu_ref, w_ref, o_ref):
    @pl.when(pl.program_id(0) < nu_ref[0])
    def _():
        x = x_ref[...].astype(_BF16)
        g = jnp.dot(x, wg_ref[...], preferred_element_type=_F32)
        u = jnp.dot(x, wu_ref[...], preferred_element_type=_F32)
        o_ref[...] = (g * jax.nn.sigmoid(g) * u * w_ref[:, 0:1]).astype(o_ref.dtype)

    @pl.when(pl.program_id(0) >= nu_ref[0])
    def _():
        o_ref[...] = jnp.zeros(o_ref.shape, o_ref.dtype)


def _swiglu_grouped(xs, wg, wu, wrow, tile_e, n_used, bm, bn=512):
    n, d = xs.shape
    ff = wg.shape[2]
    w_spec = pl.BlockSpec((None, d, bn), lambda i, j, te, nu: (te[i], 0, j))
    return pl.pallas_call(
        _gswiglu_kernel,
        grid_spec=pltpu.PrefetchScalarGridSpec(
            num_scalar_prefetch=2, grid=(n // bm, ff // bn),
            in_specs=[pl.BlockSpec((bm, d), lambda i, j, te, nu: (i, 0)), w_spec, w_spec,
                      pl.BlockSpec((bm, V7X_LANES), lambda i, j, te, nu: (i, 0))],
            out_specs=pl.BlockSpec((bm, bn), lambda i, j, te, nu: (i, j))),
        out_shape=jax.ShapeDtypeStruct((n, ff), _BF16),
        compiler_params=pltpu.CompilerParams(
            dimension_semantics=("parallel", "arbitrary"),
            vmem_limit_bytes=_vmem_limit(2 * bm * d * 4, bm * d * 2, 4 * d * bn * 2, 2 * bm * bn * 2, 4 * bm * bn * 4)),
        name="swiglu_grouped",
    )(tile_e, n_used, xs, wg, wu, wrow)


def _gdown_kernel(te_ref, nu_ref, h_ref, w_ref, o_ref):
    @pl.when(pl.program_id(0) < nu_ref[0])
    def _():
        o_ref[...] = jnp.dot(h_ref[...], w_ref[...], preferred_element_type=_F32)

    @pl.when(pl.program_id(0) >= nu_ref[0])
    def _():
        o_ref[...] = jnp.zeros(o_ref.shape, o_ref.dtype)


def _down_grouped(hid, wd, tile_e, n_used, bm, bn=1024):
    n, ff = hid.shape
    d = wd.shape[2]
    return pl.pallas_call(
        _gdown_kernel,
        grid_spec=pltpu.PrefetchScalarGridSpec(
            num_scalar_prefetch=2, grid=(n // bm, d // bn),
            in_specs=[pl.BlockSpec((bm, ff), lambda i, j, te, nu: (i, 0)),
                      pl.BlockSpec((None, ff, bn), lambda i, j, te, nu: (te[i], 0, j))],
            out_specs=pl.BlockSpec((bm, bn), lambda i, j, te, nu: (i, j))),
        out_shape=jax.ShapeDtypeStruct((n, d), _F32),
        compiler_params=pltpu.CompilerParams(
            dimension_semantics=("parallel", "arbitrary"),
            vmem_limit_bytes=_vmem_limit(2 * bm * ff * 2, 2 * ff * bn * 2, 3 * bm * bn * 4)),
        name="down_grouped",
    )(tile_e, n_used, hid, wd)


def _combine_kernel(d_ref, y_hbm, x_ref, g_ref, nw_ref, o_ref, buf, sem, *, bt, m, final_norm):
    base = pl.program_id(0) * bt

    def row_copy(k, r, src_row):
        return pltpu.make_async_copy(y_hbm.at[pl.ds(src_row, 1)], buf.at[k, pl.ds(r, 1)], sem)

    def issue(r, c):
        for k in range(TOP_K):
            row_copy(k, r, d_ref[k * m + base + r]).start()
        return c

    def drain(r, c):
        for k in range(TOP_K):
            row_copy(k, r, 0).wait()
        return c

    lax.fori_loop(0, bt, issue, 0, unroll=8)
    lax.fori_loop(0, bt, drain, 0, unroll=8)
    y = x_ref[...] + g_ref[0] * (buf[0] + buf[1])
    if final_norm:
        y = y * lax.rsqrt(jnp.mean(y * y, axis=-1, keepdims=True) + EPS) * nw_ref[...]
    o_ref[...] = y


def _combine(ys, dest, x, gate, rows_per_gate, norm_w=None, bt=256):
    m, d = x.shape
    assert m % bt == 0 and rows_per_gate % bt == 0
    per = rows_per_gate // bt
    final_norm = norm_w is not None
    nw = norm_w.reshape(1, d) if final_norm else jnp.ones((1, d), _F32)
    return pl.pallas_call(
        functools.partial(_combine_kernel, bt=bt, m=m, final_norm=final_norm),
        grid_spec=pltpu.PrefetchScalarGridSpec(
            num_scalar_prefetch=1, grid=(m // bt,),
            in_specs=[pl.BlockSpec(memory_space=pl.ANY),
                      pl.BlockSpec((bt, d), lambda i, dr: (i, 0)),
                      pl.BlockSpec((1, 1, d), lambda i, dr: (i // per, 0, 0)),
                      pl.BlockSpec((1, d), lambda i, dr: (0, 0))],
            out_specs=pl.BlockSpec((bt, d), lambda i, dr: (i, 0)),
            scratch_shapes=[pltpu.VMEM((TOP_K, bt, d), _F32), pltpu.SemaphoreType.DMA(())]),
        out_shape=jax.ShapeDtypeStruct((m, d), _F32),
        compiler_params=pltpu.CompilerParams(
            dimension_semantics=("arbitrary",),
            vmem_limit_bytes=_vmem_limit(TOP_K * bt * d * 4, 4 * bt * d * 4, 2 * bt * d * 4)),
        name="moe_combine",
    )(dest, ys, x, gate, nw)


def _moe(h, route, x, gate, rows_per_gate, wg, wu, wd, norm_w=None):
    src, wrow, tile_e, n_used, dest = _route_plan(route, MOE_TILE)
    xs = _gather_rows(h, src)
    wrow = jnp.broadcast_to(wrow[:, None], (wrow.shape[0], V7X_LANES))
    hid = _swiglu_grouped(xs, wg, wu, wrow, tile_e, n_used, MOE_TILE)
    ys = _down_grouped(hid, wd, tile_e, n_used, MOE_TILE)
    return _combine(ys, dest, x, gate, rows_per_gate, norm_w)


def _merge_kernel(yhg_ref, yml_ref, yhy_ref, w0_ref, w1_ref, w2_ref, w3_ref, m0_ref, m1_ref, m2_ref, o_ref):
    half = yml_ref.shape[1] // 2
    a_hg = jnp.dot(yhg_ref[...], w0_ref[...], preferred_element_type=_F32)
    a_ml = jnp.dot(yml_ref[:, :half], w1_ref[...], preferred_element_type=_F32) \
        + jnp.dot(yml_ref[:, half:], w2_ref[...], preferred_element_type=_F32)
    a_hy = jnp.dot(yhy_ref[...], w3_ref[...], preferred_element_type=_F32)
    sig = lambda r: jax.nn.sigmoid(r[...].astype(_F32))
    o_ref[...] = (sig(m0_ref) * a_hg + sig(m1_ref) * a_ml + sig(m2_ref) * a_hy).astype(o_ref.dtype)


def _merge(y_hg, y_ml, y_hy, wb, p2d, merge_off, bm=1024, bn=512):
    m = y_hg.shape[0]
    d = wb.shape[1]
    bm = min(bm, m)
    kb = HG_WIDTH
    assert m % bm == 0 and d % bn == 0 and ML_WIDTH == 2 * kb and HY_WIDTH == kb and merge_off % bn == 0
    y_spec = lambda w: pl.BlockSpec((bm, w), lambda i, j: (i, 0))
    w_spec = lambda r: pl.BlockSpec((kb, bn), lambda i, j, r=r: (r, j))
    g_spec = lambda b: pl.BlockSpec((bm, bn), lambda i, j, b=b: (i, (merge_off + b * d) // bn + j))
    return pl.pallas_call(
        _merge_kernel,
        grid=(m // bm, d // bn),
        in_specs=[y_spec(HG_WIDTH), y_spec(ML_WIDTH), y_spec(HY_WIDTH)] + [w_spec(r) for r in range(4)]
        + [g_spec(b) for b in range(N_BRANCH)],
        out_specs=pl.BlockSpec((bm, bn), lambda i, j: (i, j)),
        out_shape=jax.ShapeDtypeStruct((m, d), _BF16),
        compiler_params=pltpu.CompilerParams(
            dimension_semantics=("parallel", "parallel"),
            vmem_limit_bytes=_vmem_limit(2 * bm * 4 * kb * 2, 8 * kb * bn * 2, 6 * bm * bn * 2, 2 * bm * bn * 2,
                                         4 * bm * bn * 4)),
        name="merge",
    )(y_hg, y_ml, y_hy, wb, wb, wb, wb, p2d, p2d, p2d)


def _ada_kernel(c_ref, w_ref, b_ref, o_ref):
    c = c_ref[...]
    a = (c * jax.nn.sigmoid(c)).astype(_BF16)
    o_ref[...] = jnp.dot(a, w_ref[...].astype(_BF16), preferred_element_type=_F32) + b_ref[0]


def _ada(cond, ada_w, ada_b, layer, bn=512):
    rows = cond.shape[0]
    n = N_MOD * D_MODEL
    return pl.pallas_call(
        _ada_kernel,
        grid=(n // bn,),
        in_specs=[pl.BlockSpec((rows, D_MODEL), lambda j: (0, 0)),
                  pl.BlockSpec((None, D_MODEL, bn), lambda j: (layer, 0, j)),
                  pl.BlockSpec((None, 1, bn), lambda j: (layer, 0, j))],
        out_specs=pl.BlockSpec((rows, bn), lambda j: (0, j)),
        out_shape=jax.ShapeDtypeStruct((rows, n), _F32),
        compiler_params=pltpu.CompilerParams(
            dimension_semantics=("parallel",),
            vmem_limit_bytes=_vmem_limit(2 * D_MODEL * bn * 4, D_MODEL * bn * 2)),
        name="ada",
    )(cond, ada_w, ada_b.reshape(DEPTH, 1, n))


def _modnorm_kernel(*refs, with_router):
    if with_router:
        x_ref, w_ref, sh_ref, sc_ref, r_ref, h_ref, comb_ref = refs
    else:
        x_ref, w_ref, sh_ref, sc_ref, h_ref = refs
    x = x_ref[0]
    y = x * lax.rsqrt(jnp.mean(x * x, axis=-1, keepdims=True) + EPS)
    h = (y * w_ref[0]) * (1.0 + sc_ref[0]) + sh_ref[0]
    h_ref[...] = h.astype(h_ref.dtype)
    if with_router:
        logits = jnp.dot(h, r_ref[...], preferred_element_type=_F32, precision=lax.Precision.HIGHEST)
        lane = lax.broadcasted_iota(jnp.int32, logits.shape, 1)
        valid = lane < N_EXPERTS
        lg = jnp.where(valid, logits, -jnp.inf)
        v1 = jnp.max(lg, axis=-1, keepdims=True)
        i1 = jnp.min(jnp.where(lg == v1, lane, V7X_LANES), axis=-1, keepdims=True)
        lg2 = jnp.where(lane == i1, -jnp.inf, lg)
        v2 = jnp.max(lg2, axis=-1, keepdims=True)
        i2 = jnp.min(jnp.where(lg2 == v2, lane, V7X_LANES), axis=-1, keepdims=True)
        e2 = jnp.exp(v2 - v1)
        p1 = 1.0 / (1.0 + e2)
        p2 = e2 / (1.0 + e2)
        comb_ref[...] = (jnp.where(lane == N_EXPERTS, i1.astype(_F32), 0.0)
                         + jnp.where(lane == N_EXPERTS + 1, i2.astype(_F32), 0.0)
                         + jnp.where(lane == N_EXPERTS + 2, p1, 0.0) + jnp.where(lane == N_EXPERTS + 3, p2, 0.0))


def _modnorm(x, norm_w, shift, scale, router=None, bl=256):
    b, L, d = x.shape
    bl = min(bl, L)
    assert L % bl == 0
    nl = L // bl
    per_batch = shift.shape[0] == b and b > 1
    mod_map = (lambda i, j: (i, 0, 0)) if per_batch else (lambda i, j: (0, 0, 0))
    in_specs = [pl.BlockSpec((1, bl, d), lambda i, j: (i, j, 0)),
                pl.BlockSpec((1, d), lambda i, j: (0, 0)),
                pl.BlockSpec((1, 1, d), mod_map),
                pl.BlockSpec((1, 1, d), mod_map)]
    args = [x, norm_w.reshape(1, d), shift, scale]
    out_specs = [pl.BlockSpec((bl, d), lambda i, j: (i * nl + j, 0))]
    with_router = router is not None
    out_shape = [jax.ShapeDtypeStruct((b * L, d), _F32 if with_router else _BF16)]
    if with_router:
        in_specs.append(pl.BlockSpec((d, V7X_LANES), lambda i, j: (0, 0)))
        args.append(router)
        out_specs.append(pl.BlockSpec((bl, V7X_LANES), lambda i, j: (i * nl + j, 0)))
        out_shape.append(jax.ShapeDtypeStruct((b * L, V7X_LANES), _F32))
    out = pl.pallas_call(
        functools.partial(_modnorm_kernel, with_router=with_router),
        grid=(b, nl),
        in_specs=in_specs,
        out_specs=out_specs,
        out_shape=out_shape,
        compiler_params=pltpu.CompilerParams(
            dimension_semantics=("parallel", "parallel"),
            vmem_limit_bytes=_vmem_limit(2 * bl * d * 4, 2 * bl * d * 2, 4 * bl * d * 4, 2 * d * V7X_LANES * 4)),
        name="modnorm",
    )(*args)
    return out if with_router else out[0]


def _final_norm_kernel(x_ref, w_ref, o_ref):
    x = x_ref[...]
    o_ref[...] = x * lax.rsqrt(jnp.mean(x * x, axis=-1, keepdims=True) + EPS) * w_ref[...]


def _final_norm(x, w, bl=256):
    m, d = x.shape
    return pl.pallas_call(
        _final_norm_kernel,
        grid=(m // bl,),
        in_specs=[pl.BlockSpec((bl, d), lambda i: (i, 0)), pl.BlockSpec((1, d), lambda i: (0, 0))],
        out_specs=pl.BlockSpec((bl, d), lambda i: (i, 0)),
        out_shape=jax.ShapeDtypeStruct((m, d), _F32),
        compiler_params=pltpu.CompilerParams(dimension_semantics=("parallel",),
                                             vmem_limit_bytes=_vmem_limit(6 * bl * d * 4)),
        name="final_norm",
    )(x, w.reshape(1, d))


_O_HF, _O_HB, _O_HGI, _O_MLK, _O_MLV = 0, 1024, 2048, 3072, 5120
_O_HGQ, _O_HGOG, _O_MLQ, _O_MLOG, _O_HY, _O_MERGE = 7168, 8192, 9216, 11264, 13312, 16384

HG_DK = HG_WIDTH // HG_HEADS
HG_CHUNK = 64
HG_SUB = 16
HG_SPAN_MAX = 60.0
HG_BLOCK = 256
ML_CHUNK = 256


def _log_sigmoid(x):
    return jnp.minimum(x, 0.0) - jnp.log(1.0 + jnp.exp(-jnp.abs(x)))


def _silu(x):
    return x * jax.nn.sigmoid(x)


def _logaddexp(a, b):
    return jnp.maximum(a, b) + jnp.log(1.0 + jnp.exp(-jnp.abs(a - b)))


def _hgrn2_kernel(*refs, rev, with_out, finalize, has_init, n_chunks):
    refs = list(refs)
    pre_ref, v_ref = refs.pop(0), refs.pop(0)
    q_ref = refs.pop(0) if with_out else None
    lbc_ref = refs.pop(0)
    s0_ref = refs.pop(0) if has_init else None
    if finalize:
        ofwd_ref, og_ref, nw_ref = refs.pop(0), refs.pop(0), refs.pop(0)
    o_ref = refs.pop(0) if with_out else None
    st_ref = refs.pop(0)
    st_sc = refs.pop(0) if with_out else None
    C, SUB = HG_CHUNK, HG_SUB
    n_sub = C // SUB

    @pl.when(pl.program_id(2) == 0)
    def _():
        st_ref[0, 0] = s0_ref[0, 0] if has_init else jnp.zeros(st_ref.shape[2:], _F32)

    lbc = lbc_ref[...]
    log_lb, log_1mlb, one_m_lb = lbc[0:1], lbc[1:2], lbc[2:3]
    r_i = lax.broadcasted_iota(jnp.int32, (C, C), 0)
    c_i = lax.broadcasted_iota(jnp.int32, (C, C), 1)
    tri = ((c_i >= r_i) if rev else (c_i <= r_i)).astype(_F32)
    row64 = lax.broadcasted_iota(jnp.int32, (C, 1), 0)
    row16 = lax.broadcasted_iota(jnp.int32, (SUB, 1), 0)
    lane64 = lax.broadcasted_iota(jnp.int32, (SUB, C), 1)
    last = 0 if rev else C - 1

    tb = n_chunks * C
    rb = lax.broadcasted_iota(jnp.int32, (tb, tb), 0)
    cb = lax.broadcasted_iota(jnp.int32, (tb, tb), 1)
    tri_blk = (((rb // C) == (cb // C)) & ((cb >= rb) if rev else (cb <= rb))).astype(_BF16)

    def block_gates():
        pre = pre_ref[0].astype(_F32)
        g = _logaddexp(log_lb, log_1mlb + _log_sigmoid(pre))
        kk = one_m_lb * jax.nn.sigmoid(-pre)
        g_hi = g.astype(_BF16)
        r1 = g - g_hi.astype(_F32)
        g_mid = r1.astype(_BF16)
        g_lo = (r1 - g_mid.astype(_F32)).astype(_BF16)
        b3 = jnp.dot(tri_blk, jnp.concatenate([g_hi, g_mid, g_lo], axis=1), preferred_element_type=_F32)
        b = b3[:, :HG_DK] + b3[:, HG_DK:2 * HG_DK] + b3[:, 2 * HG_DK:]
        return g, kk, b

    def chunk_terms(c, gates):
        rows = slice(c * C, (c + 1) * C)
        g, kk, b = (a[rows] for a in gates)
        v = v_ref[0, rows, :]
        b_end = b[last:last + 1]
        kt = (kk * jnp.exp(b_end - b)).astype(_BF16)
        upd = lax.dot_general(v, kt, (((0,), (0,)), ((), ())), preferred_element_type=_F32)
        if not with_out:
            return jnp.exp(b_end), upd, None, None
        q = _silu(q_ref[0, rows, :].astype(_F32))
        qe = (q * jnp.exp(b)).astype(_BF16)
        return jnp.exp(b_end), upd, qe, (q, kk, b, b - g, v)

    def pivot(bx, i):
        first = i * SUB + (SUB - 1 if rev else 0)
        return bx[first:first + 1]

    def intra_exact(q, kk, b, bx):
        atts = []
        for i in range(n_sub):
            sub = slice(i * SUB, (i + 1) * SUB)
            piv = pivot(bx, i)
            qi = q[sub] * jnp.exp(b[sub] - piv)
            earlier = (row64 >= (i + 1) * SUB) if rev else (row64 < i * SUB)
            khat = jnp.where(earlier, kk * jnp.exp(jnp.where(earlier, piv - b, 0.0)), 0.0)
            att = lax.dot_general(qi.astype(_BF16), khat.astype(_BF16),
                                  (((1,), (1,)), ((), ())), preferred_element_type=_F32)
            for s in range(SUB):
                r = i * SUB + s
                valid = (row16 <= s) if rev else (row16 >= s)
                rel = jnp.where(valid, b[sub] - b[r:r + 1], 0.0)
                z = jnp.where(valid, q[sub] * jnp.exp(rel) * kk[r:r + 1], 0.0)
                att = jnp.where(lane64 == r, jnp.sum(z, axis=1, keepdims=True), att)
            atts.append(att)
        return jnp.concatenate(atts, axis=0)

    def intra_factored(q, kk, b, bx):
        piv_rows = jnp.concatenate([jnp.broadcast_to(pivot(bx, i), (SUB, HG_DK)) for i in range(n_sub)], axis=0)
        qhat = q * jnp.exp(b - piv_rows)
        lhs, rhs = [], []
        for i in range(n_sub):
            in_sub = (row64 >= i * SUB) & (row64 < (i + 1) * SUB)
            lhs.append(jnp.where(in_sub, qhat, 0.0).astype(_BF16))
            upto = (row64 >= i * SUB) if rev else (row64 < (i + 1) * SUB)
            e = jnp.minimum(jnp.where(upto, pivot(bx, i) - b, 0.0), HG_SPAN_MAX)
            rhs.append(jnp.where(upto, kk * jnp.exp(e), 0.0).astype(_BF16))
        att = lax.dot_general(jnp.concatenate(lhs, axis=1), jnp.concatenate(rhs, axis=1),
                              (((1,), (1,)), ((), ())), preferred_element_type=_F32)
        return jnp.where(tri > 0.0, att, 0.0)

    def write_out(outs):
        o = jnp.concatenate(outs, axis=0)
        if finalize:
            y = ofwd_ref[0] + o
            y = y * lax.rsqrt(jnp.mean(y * y, axis=-1, keepdims=True) + EPS) * nw_ref[...]
            o_ref[0] = (y * _silu(og_ref[0].astype(_F32))).astype(o_ref.dtype)
        else:
            o_ref[0] = o

    def chunk_out(intra, term, st):
        _, _, qe, (q, kk, b, bx, v) = term
        return jnp.dot(intra(q, kk, b, bx).astype(_BF16), v, preferred_element_type=_F32) \
            + lax.dot_general(qe, st.astype(_BF16), (((1,), (1,)), ((), ())), preferred_element_type=_F32)

    gates = block_gates()
    terms = [chunk_terms(c, gates) for c in range(n_chunks)]
    st = st_ref[0, 0]
    outs = [None] * n_chunks
    for c in (reversed(range(n_chunks)) if rev else range(n_chunks)):
        if with_out:
            st_sc[c] = st
            outs[c] = chunk_out(intra_factored, terms[c], st)
        st = st * terms[c][0] + terms[c][1]
    st_ref[0, 0] = st
    if not with_out:
        return
    write_out(outs)
    spans = []
    for term in terms:
        b, bx = term[3][2], term[3][3]
        for i in range(n_sub):
            last_i = i * SUB + (0 if rev else SUB - 1)
            spans.append(pivot(bx, i) - b[last_i:last_i + 1])
    span = jnp.max(jnp.concatenate(spans, axis=0))

    @pl.when(span > HG_SPAN_MAX)
    def _():
        write_out([chunk_out(intra_exact, chunk_terms(c, gates), st_sc[c]) for c in range(n_chunks)])


def _hgrn2(p, lbc, *, d, with_out, init=None, o_fwd=None, norm_w=None):
    bsz, L, _ = p.shape
    rev = d == 1
    tb = min(L, HG_BLOCK)
    assert L % tb == 0 and tb % HG_CHUNK == 0
    nblk = L // tb
    finalize = o_fwd is not None
    has_init = init is not None
    blk = (lambda n: nblk - 1 - n) if rev else (lambda n: n)
    pre_off = (_O_HB if rev else _O_HF) // HG_DK

    def tok_spec(off):
        return pl.BlockSpec((1, tb, HG_DK), lambda b, h, n: (b, blk(n), off + h))

    in_specs = [tok_spec(pre_off), tok_spec(_O_HGI // HG_DK)]
    args = [p, p]
    if with_out:
        in_specs.append(tok_spec(_O_HGQ // HG_DK))
        args.append(p)
    in_specs.append(pl.BlockSpec((None, 8, HG_DK), lambda b, h, n: (d, 0, h)))
    args.append(lbc)
    state_spec = pl.BlockSpec((1, 1, HG_DK, HG_DK), lambda b, h, n: (b, h, 0, 0))
    if has_init:
        in_specs.append(state_spec)
        args.append(init)
    if finalize:
        in_specs += [tok_spec(0), tok_spec(_O_HGOG // HG_DK), pl.BlockSpec((1, HG_DK), lambda b, h, n: (0, h))]
        args += [o_fwd, p, norm_w.reshape(1, HG_WIDTH)]
    out_specs, out_shape = [], []
    if with_out:
        out_specs.append(tok_spec(0))
        out_shape.append(jax.ShapeDtypeStruct((bsz, L, HG_WIDTH), _BF16 if finalize else _F32))
    out_specs.append(state_spec)
    out_shape.append(jax.ShapeDtypeStruct((bsz, HG_HEADS, HG_DK, HG_DK), _F32))
    out = pl.pallas_call(
        functools.partial(_hgrn2_kernel, rev=rev, with_out=with_out, finalize=finalize, has_init=has_init,
                          n_chunks=tb // HG_CHUNK),
        grid=(bsz, HG_HEADS, nblk),
        in_specs=in_specs, out_specs=out_specs, out_shape=out_shape,
        scratch_shapes=[pltpu.VMEM((tb // HG_CHUNK, HG_DK, HG_DK), _F32)] if with_out else [],
        compiler_params=pltpu.CompilerParams(dimension_semantics=("parallel", "parallel", "arbitrary")),
        name="hgrn2",
    )(*args)
    return (out[0], out[1]) if with_out else (None, out[0])


def _mlstm_kernel(*refs, rev, with_out, finalize, has_init, seg):
    refs = list(refs)
    k_ref, v_ref = refs.pop(0), refs.pop(0)
    q_ref = refs.pop(0) if with_out else None
    g_ref, ckw_ref, ckb_ref = refs.pop(0), refs.pop(0), refs.pop(0)
    if with_out:
        cqw_ref, cqb_ref = refs.pop(0), refs.pop(0)
    if has_init:
        c0_ref, n0_ref, m0_ref = refs.pop(0), refs.pop(0), refs.pop(0)
    if finalize:
        ofwd_ref, og_ref, nw_ref = refs.pop(0), refs.pop(0), refs.pop(0)
    o_ref = refs.pop(0) if with_out else None
    c_ref, n_ref, m_ref = refs
    C = k_ref.shape[1]

    @pl.when(pl.program_id(2) == 0)
    def _():
        if has_init:
            c_ref[...] = c0_ref[...]
            n_ref[...] = n0_ref[...]
            m_ref[...] = m0_ref[...]
        else:
            c_ref[...] = jnp.zeros(c_ref.shape, _F32)
            n_ref[...] = jnp.zeros(n_ref.shape, _F32)
            m_ref[...] = jnp.zeros(m_ref.shape, _F32)

    g = g_ref[0, 0]
    ji = 2 if rev else 0
    i_row = g[ji:ji + 1]
    f_row = _log_sigmoid(g[ji + 1:ji + 2])
    r_i = lax.broadcasted_iota(jnp.int32, (C, C), 0)
    c_i = lax.broadcasted_iota(jnp.int32, (C, C), 1)
    mask = (c_i >= r_i) if rev else (c_i <= r_i)
    upto = ((r_i >= c_i) if rev else (r_i <= c_i)).astype(_F32)
    b_row = jnp.dot(jnp.broadcast_to(f_row, (8, C)), upto, preferred_element_type=_F32,
                    precision=lax.Precision.HIGHEST)[0:1]
    b_col = jnp.sum(jnp.where(mask, f_row, 0.0), axis=1, keepdims=True)
    i_col = jnp.sum(jnp.where(r_i == c_i, i_row, 0.0), axis=1, keepdims=True)
    b_end = jnp.sum(f_row, axis=1, keepdims=True)
    m_prev = m_ref[0, 0][0:1, 0:1]
    cm = c_ref[0, 0]
    nrow = n_ref[0, 0]

    pos = lax.broadcasted_iota(jnp.int32, (C, 1), 0) % seg

    def conv(x_ref, w_ref, b_ref):
        x = x_ref[0].astype(_F32)
        w = w_ref[...]
        x_prev = jnp.where(pos == 0, 0.0, pltpu.roll(x, 1, 0))
        x_next = jnp.where(pos == seg - 1, 0.0, pltpu.roll(x, C - 1, 0))
        return b_ref[...] + w[0:1] * x_prev + w[1:2] * x + w[2:3] * x_next

    k = _silu(conv(k_ref, ckw_ref, ckb_ref)) * ML_DH ** -0.5
    v = v_ref[0]

    if with_out:
        q = _silu(conv(q_ref, cqw_ref, cqb_ref))
        qb = q.astype(_BF16)
        logd = b_col - b_row + i_row
        m_t = jnp.maximum(b_col + m_prev, jnp.max(jnp.where(mask, logd, NEG_BIG), axis=1, keepdims=True))
        prev = jnp.exp(b_col + m_prev - m_t)
        dmat = jnp.where(mask, jnp.exp(jnp.where(mask, logd - m_t, 0.0)), 0.0)
        s = lax.dot_general(qb, k.astype(_BF16), (((1,), (1,)), ((), ())), preferred_element_type=_F32) * dmat
        num = prev * jnp.dot(qb, cm.astype(_BF16), preferred_element_type=_F32) \
            + jnp.dot(s.astype(_BF16), v, preferred_element_type=_F32)
        den = prev * jnp.sum(q * nrow, axis=1, keepdims=True) + jnp.sum(s, axis=1, keepdims=True)
        h = num / jnp.maximum(jnp.abs(den), jnp.exp(-m_t))
        if finalize:
            y = ofwd_ref[0] + h
            y = y * lax.rsqrt(jnp.mean(y * y, axis=-1, keepdims=True) + EPS) * nw_ref[...]
            o_ref[0] = (y * jax.nn.sigmoid(og_ref[0].astype(_F32))).astype(o_ref.dtype)
        else:
            o_ref[0] = h

    log_w = b_end - b_col + i_col
    m_new = jnp.maximum(b_end + m_prev, jnp.max(log_w, axis=0, keepdims=True))
    decay = jnp.exp(b_end + m_prev - m_new)
    wk = jnp.exp(log_w - m_new) * k
    c_ref[0, 0] = decay * cm + lax.dot_general(wk.astype(_BF16), v, (((0,), (0,)), ((), ())),
                                               preferred_element_type=_F32)
    n_ref[0, 0] = decay * nrow + jnp.sum(wk, axis=0, keepdims=True)
    m_ref[0, 0] = jnp.broadcast_to(m_new, m_ref.shape[2:])


def _mlstm(p, gate_rows, conv_w, conv_b, layer, *, d, seg, with_out, init=None, o_fwd=None, norm_w=None):
    bsz, L, _ = p.shape
    rev = d == 1
    C = min(L, ML_CHUNK)
    assert L % C == 0 and C % seg == 0
    nch = L // C
    finalize = o_fwd is not None
    has_init = init is not None
    blk = (lambda n: nch - 1 - n) if rev else (lambda n: n)

    def tok_spec(off):
        return pl.BlockSpec((1, C, ML_DH), lambda b, h, n: (b, blk(n), off + h))

    def conv_specs(off):
        return [pl.BlockSpec((None, SHORT_CONV, ML_DH), lambda b, h, n: (layer, 0, off + h)),
                pl.BlockSpec((None, 1, ML_DH), lambda b, h, n: (layer, 0, off + h))]

    conv_b3 = conv_b.reshape(DEPTH, 1, 2 * ML_WIDTH)
    in_specs = [tok_spec(_O_MLK // ML_DH), tok_spec(_O_MLV // ML_DH)]
    args = [p, p]
    if with_out:
        in_specs.append(tok_spec(_O_MLQ // ML_DH))
        args.append(p)
    in_specs.append(pl.BlockSpec((1, 1, 8, C), lambda b, h, n: (b, h, 0, blk(n))))
    args.append(gate_rows)
    in_specs += conv_specs(0)
    args += [conv_w, conv_b3]
    if with_out:
        in_specs += conv_specs(ML_HEADS)
        args += [conv_w, conv_b3]
    state_specs = [pl.BlockSpec((1, 1, ML_DH, ML_DH), lambda b, h, n: (b, h, 0, 0)),
                   pl.BlockSpec((1, 1, 1, ML_DH), lambda b, h, n: (b, h, 0, 0)),
                   pl.BlockSpec((1, 1, 8, V7X_LANES), lambda b, h, n: (b, h, 0, 0))]
    state_shape = [jax.ShapeDtypeStruct((bsz, ML_HEADS, ML_DH, ML_DH), _F32),
                   jax.ShapeDtypeStruct((bsz, ML_HEADS, 1, ML_DH), _F32),
                   jax.ShapeDtypeStruct((bsz, ML_HEADS, 8, V7X_LANES), _F32)]
    if has_init:
        in_specs += state_specs
        args += list(init)
    if finalize:
        in_specs += [tok_spec(0), tok_spec(_O_MLOG // ML_DH), pl.BlockSpec((1, ML_DH), lambda b, h, n: (0, h))]
        args += [o_fwd, p, norm_w.reshape(1, ML_WIDTH)]
    out_specs, out_shape = [], []
    if with_out:
        out_specs.append(tok_spec(0))
        out_shape.append(jax.ShapeDtypeStruct((bsz, L, ML_WIDTH), _BF16 if finalize else _F32))
    out = pl.pallas_call(
        functools.partial(_mlstm_kernel, rev=rev, with_out=with_out, finalize=finalize, has_init=has_init, seg=seg),
        grid=(bsz, ML_HEADS, nch),
        in_specs=in_specs, out_specs=out_specs + state_specs, out_shape=out_shape + state_shape,
        compiler_params=pltpu.CompilerParams(dimension_semantics=("parallel", "parallel", "arbitrary")),
        name="mlstm",
    )(*args)
    return (out[0], tuple(out[1:])) if with_out else (None, tuple(out))


def _dft_kernel(c_ref, s_ref, st_ref, *, n_fft, bm):
    shape = c_ref.shape
    row = lax.broadcasted_iota(jnp.int32, shape, 0) + pl.program_id(0) * bm
    col = lax.broadcasted_iota(jnp.int32, shape, 1)
    ang = ((row * col) & (n_fft - 1)).astype(_F32) * (2.0 * math.pi / n_fft)
    alt_col = jnp.where((col & 1) == 0, 1.0, -1.0)
    alt_row = jnp.where((row & 1) == 0, 1.0, -1.0)
    msin = -jnp.sin(ang)
    c_ref[...] = jnp.cos(ang).astype(c_ref.dtype)
    s_ref[...] = jnp.where(row == 0, alt_col, msin).astype(s_ref.dtype)
    st_ref[...] = jnp.where(col == 0, alt_row, msin).astype(st_ref.dtype)


def _dft_mats(L, bm=256):
    bm = min(bm, L)
    spec = pl.BlockSpec((bm, L), lambda i: (i, 0))
    shape = jax.ShapeDtypeStruct((L, L), _BF16)
    return pl.pallas_call(
        functools.partial(_dft_kernel, n_fft=2 * L, bm=bm),
        grid=(L // bm,), in_specs=[], out_specs=[spec] * 3, out_shape=[shape] * 3,
        compiler_params=pltpu.CompilerParams(dimension_semantics=("parallel",)),
        name="dft_mats",
    )()


def _hy_filter_kernel(band_ref, w1t_ref, w1c_ref, w1s_ref, b1_ref, w2_ref, b2_ref, w3_ref, fr_ref, dl_ref,
                      klo_ref, khi_ref, asum_ref, *, L, bl):
    hp = lax.Precision.HIGHEST
    m = (lax.broadcasted_iota(jnp.int32, (bl, 1), 0) + pl.program_id(0) * bl)
    half = HY_ORDER * HY_WIDTH
    total = jnp.zeros((1, half), _F32)
    for side, out_ref in ((0, klo_ref), (1, khi_ref)):
        pos = (m if side == 0 else (L - 1 - m)).astype(_F32)
        t = pos / (L - 1)
        ang = (2.0 * math.pi / L) * pos * band_ref[...]
        a = t * w1t_ref[...] + jnp.dot(jnp.cos(ang), w1c_ref[...], precision=hp, preferred_element_type=_F32) \
            + jnp.dot(jnp.sin(ang), w1s_ref[...], precision=hp, preferred_element_type=_F32) + b1_ref[...]
        a = jnp.sin(fr_ref[0:1] * a)
        a = jnp.sin(fr_ref[1:2] * (jnp.dot(a, w2_ref[...], precision=hp, preferred_element_type=_F32) + b2_ref[...]))
        f = jnp.dot(a, w3_ref[:, side * half:(side + 1) * half], precision=hp, preferred_element_type=_F32)
        f = f * jnp.exp(-t * dl_ref[...])
        if side == 1:
            f = jnp.where(m == 0, 0.0, f)
        out_ref[...] = f.astype(out_ref.dtype)
        total = total + jnp.sum(jnp.abs(f), axis=0, keepdims=True)

    @pl.when(pl.program_id(0) == 0)
    def _():
        asum_ref[...] = jnp.zeros(asum_ref.shape, _F32)

    asum_ref[...] += jnp.broadcast_to(total, asum_ref.shape)


def _hy_filter(L, w1, b1, w2, b2, w3, freq, bl=256):
    bl = min(bl, L)
    ffn = w1.shape[1]
    half = HY_ORDER * HY_WIDTH
    bands = jnp.pad(jnp.linspace(1e-4, HY_BANDS - 1, HY_BANDS, dtype=_F32), (0, V7X_LANES - HY_BANDS)).reshape(1, -1)
    pad_rows = ((0, V7X_LANES - HY_BANDS), (0, 0))
    deltas = jnp.abs(jnp.linspace(HY_MIN_DECAY, HY_MAX_DECAY, HY_WIDTH, dtype=_F32))
    args = [bands, w1[0:1], jnp.pad(w1[1:1 + HY_BANDS], pad_rows), jnp.pad(w1[1 + HY_BANDS:], pad_rows),
            b1.reshape(1, ffn), w2, b2.reshape(1, ffn), w3, freq, jnp.tile(deltas, HY_ORDER).reshape(1, half)]
    full = lambda a: pl.BlockSpec(a.shape, lambda i: (0,) * a.ndim)
    return pl.pallas_call(
        functools.partial(_hy_filter_kernel, L=L, bl=bl),
        grid=(L // bl,),
        in_specs=[full(a) for a in args],
        out_specs=[pl.BlockSpec((bl, half), lambda i: (i, 0)), pl.BlockSpec((bl, half), lambda i: (i, 0)),
                   pl.BlockSpec((8, half), lambda i: (0, 0))],
        out_shape=[jax.ShapeDtypeStruct((L, half), _BF16), jax.ShapeDtypeStruct((L, half), _BF16),
                   jax.ShapeDtypeStruct((8, half), _F32)],
        compiler_params=pltpu.CompilerParams(dimension_semantics=("arbitrary",)),
        name="hy_filter",
    )(*args)


def _hy_kf_kernel(t_ref, klo_ref, khi_ref, asum_ref, o_ref, *, bm):
    row = lax.broadcasted_iota(jnp.int32, (bm, 1), 0) + pl.program_id(0) * bm
    sgn = jnp.where((row & 1) == 0, 1.0, -1.0)
    t = t_ref[...]
    acc = jnp.dot(t, klo_ref[...], preferred_element_type=_F32) \
        + sgn * jnp.dot(t, khi_ref[...], preferred_element_type=_F32)
    o_ref[...] = acc / (asum_ref[0:1] + EPS)


def _hy_kf(cmat, smat, klo, khi, asum, bm=512, bn=512):
    L = cmat.shape[0]
    tmat = jnp.concatenate([cmat, smat], axis=0)
    half = klo.shape[1]
    bm, bn = min(bm, L), min(bn, half)
    return pl.pallas_call(
        functools.partial(_hy_kf_kernel, bm=bm),
        grid=(2 * L // bm, half // bn),
        in_specs=[pl.BlockSpec((bm, L), lambda i, j: (i, 0)),
                  pl.BlockSpec((L, bn), lambda i, j: (0, j)), pl.BlockSpec((L, bn), lambda i, j: (0, j)),
                  pl.BlockSpec((8, bn), lambda i, j: (0, j))],
        out_specs=pl.BlockSpec((bm, bn), lambda i, j: (i, j)),
        out_shape=jax.ShapeDtypeStruct((2 * L, half), _F32),
        compiler_params=pltpu.CompilerParams(
            dimension_semantics=("parallel", "parallel"),
            vmem_limit_bytes=_vmem_limit(2 * bm * L * 2, 4 * L * bn * 2, 6 * bm * bn * 4)),
        name="hy_kf",
    )(tmat, klo, khi, asum)


def _hy_split_kernel(*refs, seg):
    p_refs, w_refs, b_refs, o_refs = refs[0:3], refs[3:6], refs[6:9], refs[9:12]
    bl = o_refs[0].shape[1]
    pos = lax.broadcasted_iota(jnp.int32, (bl, 1), 0) % seg
    for p_ref, w_ref, b_ref, o_ref in zip(p_refs, w_refs, b_refs, o_refs):
        x = p_ref[0].astype(_F32)
        w = w_ref[...]
        x_prev = jnp.where(pos == 0, 0.0, pltpu.roll(x, 1, 0))
        x_next = jnp.where(pos == seg - 1, 0.0, pltpu.roll(x, bl - 1, 0))
        o_ref[0] = (b_ref[...] + w[0:1] * x_prev + w[1:2] * x + w[2:3] * x_next).astype(o_ref.dtype)


def _hy_split(p, conv_w, conv_b, seg):
    bsz, L, _ = p.shape
    bl = min(L, 256)
    assert L % bl == 0 and bl % seg == 0
    base = _O_HY // HY_WIDTH
    out_spec = pl.BlockSpec((1, bl, HY_WIDTH), lambda b, i: (b, i, 0))
    conv_b2 = conv_b.reshape(1, 3 * HY_WIDTH)
    in_specs = [pl.BlockSpec((1, bl, HY_WIDTH), lambda b, i, j=j: (b, i, base + j)) for j in range(3)]
    in_specs += [pl.BlockSpec((SHORT_CONV, HY_WIDTH), lambda b, i, j=j: (0, j)) for j in range(3)]
    in_specs += [pl.BlockSpec((1, HY_WIDTH), lambda b, i, j=j: (0, j)) for j in range(3)]
    return pl.pallas_call(
        functools.partial(_hy_split_kernel, seg=seg),
        grid=(bsz, L // bl),
        in_specs=in_specs,
        out_specs=[out_spec] * 3,
        out_shape=[jax.ShapeDtypeStruct((bsz, L, HY_WIDTH), _BF16)] * 3,
        compiler_params=pltpu.CompilerParams(dimension_semantics=("parallel", "parallel")),
        name="hy_split",
    )(p, p, p, conv_w, conv_w, conv_w, conv_b2, conv_b2, conv_b2)


def _hy_fwd_kernel(c_ref, s_ref, z_ref, kre_ref, kim_ref, pre_ref, pim_ref, *, bm):
    z = z_ref[...]
    xre = jnp.dot(c_ref[...], z, preferred_element_type=_F32)
    xim = jnp.dot(s_ref[...], z, preferred_element_type=_F32)
    kre, kim = kre_ref[...], kim_ref[...]
    row0 = (lax.broadcasted_iota(jnp.int32, (bm, 1), 0) + pl.program_id(0) * bm) == 0
    pre = jnp.where(row0, 0.5 * xre * kre, xre * kre - xim * kim)
    pim = jnp.where(row0, 0.5 * xim * kim, xre * kim + xim * kre)
    pre_ref[...] = pre.astype(pre_ref.dtype)
    pim_ref[...] = pim.astype(pim_ref.dtype)


def _hy_fwd(cmat, smat, z, kf, order, bm=512, bn=512):
    bsz, L, width = z.shape
    bm, bn = min(bm, L), min(bn, width)
    nj = width // bn
    mat_spec = pl.BlockSpec((bm, L), lambda i, b, j: (i, 0))
    out_spec = pl.BlockSpec((None, bm, bn), lambda i, b, j: (b, i, j))
    out_shape = jax.ShapeDtypeStruct((bsz, L, width), _BF16)
    return pl.pallas_call(
        functools.partial(_hy_fwd_kernel, bm=bm),
        grid=(L // bm, bsz, nj),
        in_specs=[mat_spec, mat_spec,
                  pl.BlockSpec((None, L, bn), lambda i, b, j: (b, 0, j)),
                  pl.BlockSpec((bm, bn), lambda i, b, j: (i, order * nj + j)),
                  pl.BlockSpec((bm, bn), lambda i, b, j: (L // bm + i, order * nj + j))],
        out_specs=[out_spec, out_spec], out_shape=[out_shape, out_shape],
        compiler_params=pltpu.CompilerParams(
            dimension_semantics=("parallel", "parallel", "parallel"),
            vmem_limit_bytes=_vmem_limit(4 * bm * L * 2, 2 * L * bn * 2, 4 * bm * bn * 4, 4 * bm * bn * 2,
                                         4 * bm * bn * 4)),
        name="hy_fwd",
    )(cmat, smat, z, kf, kf)


def _hy_inv_kernel(c_ref, st_ref, pre_ref, pim_ref, z_ref, g_ref, skip_ref, o_ref, *, scale):
    conv = jnp.dot(c_ref[...], pre_ref[...], preferred_element_type=_F32) \
        + jnp.dot(st_ref[...], pim_ref[...], preferred_element_type=_F32)
    z = z_ref[...].astype(_F32)
    o_ref[...] = (g_ref[...].astype(_F32) * (scale * conv + skip_ref[...] * z)).astype(o_ref.dtype)


def _hy_inv(cmat, smat_t, pre, pim, z, gate, skip, bm=512, bn=512):
    bsz, L, width = z.shape
    bm, bn = min(bm, L), min(bn, width)
    mat_spec = pl.BlockSpec((bm, L), lambda i, b, j: (i, 0))
    spec_spec = pl.BlockSpec((None, L, bn), lambda i, b, j: (b, 0, j))
    tok_spec = pl.BlockSpec((None, bm, bn), lambda i, b, j: (b, i, j))
    return pl.pallas_call(
        functools.partial(_hy_inv_kernel, scale=1.0 / L),
        grid=(L // bm, bsz, width // bn),
        in_specs=[mat_spec, mat_spec, spec_spec, spec_spec, tok_spec, tok_spec,
                  pl.BlockSpec((1, bn), lambda i, b, j: (0, j))],
        out_specs=tok_spec,
        out_shape=jax.ShapeDtypeStruct((bsz, L, width), _BF16),
        compiler_params=pltpu.CompilerParams(
            dimension_semantics=("parallel", "parallel", "parallel"),
            vmem_limit_bytes=_vmem_limit(4 * bm * L * 2, 4 * L * bn * 2, 6 * bm * bn * 2, 3 * bm * bn * 4)),
        name="hy_inv",
    )(cmat, smat_t, pre, pim, z, gate, skip.reshape(1, width))


def _hyena(p, lp, seg, dft):
    L = p.shape[1]
    cmat, smat, smat_t = dft
    klo, khi, asum = _hy_filter(L, lp["hy_w1"], lp["hy_b1"], lp["hy_w2"], lp["hy_b2"], lp["hy_w3"], lp["hy_freq"])
    kf = _hy_kf(cmat, smat, klo, khi, asum)
    z, x1, x2 = _hy_split(p, lp["hy_conv_w"], lp["hy_conv_b"], seg)
    for o, gate in enumerate((x1, x2)):
        pre, pim = _hy_fwd(cmat, smat, z, kf, o)
        z = _hy_inv(cmat, smat_t, pre, pim, z, gate, lp["hy_skip"][o])
    return z


def _prep_layer_weights(l, w_in, w_branch, w_out):
    wl = w_in[l]
    w_main = jnp.concatenate([wl[:, :N_STATE_MAIN], wl[:, N_STATE_COLS:]], axis=1).astype(_BF16)
    w_gate = jnp.pad(wl[:, N_STATE_MAIN:N_STATE_COLS], ((0, 0), (0, V7X_LANES - N_GATE_COLS))).astype(_BF16)
    return w_main, w_gate, w_branch[l].astype(_BF16), w_out[l].astype(_BF16)


def _lower_bound_rows(lb):
    lb = lb.astype(_F32)
    rows = jnp.stack([jnp.log(jnp.maximum(lb, LB_FLOOR)), jnp.log1p(-lb), 1.0 - lb], axis=1)
    return jnp.pad(rows, ((0, 0), (0, 5), (0, 0)))


def _gate_rows(gates, gate_b):
    bsz, L, _ = gates.shape
    g = (gates[..., :N_GATE_COLS] + gate_b).reshape(bsz, L, 4, ML_HEADS)
    return jnp.pad(g.transpose(0, 3, 2, 1), ((0, 0), (0, 0), (0, 4), (0, 0)))


def _stream_mixer(xs, mods, lw, lp, layer, n_rows, init, with_out):
    w_main, w_gate, wb, wo = lw
    bsz, L, d = xs.shape
    sh, sc, g = mods[0], mods[1], mods[2]
    h = _modnorm(xs, lp["norm1"], sh, sc)
    w_proj = w_main if with_out else w_main[:, :N_STATE_MAIN]
    p = _matmul(h, w_proj, out_dtype=_BF16).reshape(bsz, L, -1)
    gates = _matmul(h, w_gate, out_dtype=_F32).reshape(bsz, L, V7X_LANES)
    grows = _gate_rows(gates, lp["ml_gate_b"])
    seg = L // n_rows
    hg_init, ml_init = init if init is not None else ((None, None), (None, None))
    hg_kw = dict(with_out=with_out)
    ml_kw = dict(seg=seg, with_out=with_out)
    o_hg, s_hg_f = _hgrn2(p, lp["lbc"], d=0, init=hg_init[0], **hg_kw)
    y_hg, s_hg_b = _hgrn2(p, lp["lbc"], d=1, init=hg_init[1], o_fwd=o_hg, norm_w=lp["hg_norm_w"], **hg_kw)
    o_ml, s_ml_f = _mlstm(p, grows, lp["ml_conv_w"], lp["ml_conv_b"], layer, d=0, init=ml_init[0], **ml_kw)
    y_ml, s_ml_b = _mlstm(p, grows, lp["ml_conv_w"], lp["ml_conv_b"], layer, d=1, init=ml_init[1],
                          o_fwd=o_ml, norm_w=lp["ml_norm_w"], **ml_kw)
    finals = ((s_hg_f, s_hg_b), (s_ml_f, s_ml_b))
    if not with_out:
        return None, finals
    y_hy = _hyena(p, lp, seg, lp["dft"][L])
    m = bsz * L
    merged = _merge(y_hg.reshape(m, -1), y_ml.reshape(m, -1), y_hy.reshape(m, -1), wb, p.reshape(m, -1), _O_MERGE)
    x_new = _matmul(merged, wo, out_dtype=_F32, res=xs.reshape(m, d), gate=g,
                    rows_per_gate=L if g.shape[0] > 1 else m)
    return x_new.reshape(bsz, L, d), finals


def _stream_ffn(xs, mods, norm2, l, ffn, final_w=None):
    bsz, L, d = xs.shape
    m = bsz * L
    sh2, sc2, g2 = mods[3], mods[4], mods[5]
    rows_per_gate = L if g2.shape[0] > 1 else m
    if l % 2 == 0:
        wg, wu, wd = ffn["dense"]
        h = _modnorm(xs, norm2, sh2, sc2)
        hid = _swiglu(h, wg, wu)
        out = _matmul(hid, wd, out_dtype=_F32, bk=DENSE_FF_PAD // 4, res=xs.reshape(m, d), gate=g2,
                      rows_per_gate=rows_per_gate)
    else:
        router, wg, wu, wd = ffn["moe"]
        h, route = _modnorm(xs, norm2, sh2, sc2, router=router)
        out = _moe(h, route, xs.reshape(m, d), g2, rows_per_gate, wg, wu, wd, final_w)
    return out.reshape(bsz, L, d)


def kernel(x, c, ctx, c_ctx, norm1_w, norm2_w, ada_w, ada_b, w_in, hg_lb_logits, hg_norm_w,
           ml_conv_w, ml_conv_b, ml_gate_b, ml_norm_w, hy_conv_w, hy_conv_b, hy_w1, hy_b1,
           hy_w2, hy_b2, hy_w3, hy_freq, hy_skip, w_branch, w_out, dense_w_gate, dense_w_up,
           dense_w_down, moe_router, moe_w_gate, moe_w_up, moe_w_down, final_norm_w):
    bsz, L, d = x.shape
    rows = L // GRID_W
    lb_w = jax.nn.softmax(hg_lb_logits.astype(_F32), axis=0)
    lower_bounds = jnp.cumsum(lb_w, axis=0) - lb_w[0]

    cond = jnp.zeros((8, d), _F32).at[:bsz].set(c).at[bsz].set(c_ctx)
    dft = {n: _dft_mats(n) for n in {L, ctx.shape[1]}}
    for l in range(DEPTH):
        lp = {
            "norm1": norm1_w[l], "hg_norm_w": hg_norm_w[l], "lbc": _lower_bound_rows(lower_bounds[l]), "dft": dft,
            "ml_conv_w": ml_conv_w, "ml_conv_b": ml_conv_b, "ml_gate_b": ml_gate_b[l],
            "ml_norm_w": ml_norm_w[l], "hy_conv_w": hy_conv_w[l], "hy_conv_b": hy_conv_b[l],
            "hy_w1": hy_w1[l], "hy_b1": hy_b1[l], "hy_w2": hy_w2[l], "hy_b2": hy_b2[l],
            "hy_w3": hy_w3[l], "hy_freq": hy_freq[l], "hy_skip": hy_skip[l],
        }
        lw = _prep_layer_weights(l, w_in, w_branch, w_out)
        if l % 2 == 0:
            i = l // 2
            pad = DENSE_FF_PAD - DENSE_FF
            ffn = {"dense": (jnp.pad(dense_w_gate[i].astype(_BF16), ((0, 0), (0, pad))),
                             jnp.pad(dense_w_up[i].astype(_BF16), ((0, 0), (0, pad))),
                             jnp.pad(dense_w_down[i].astype(_BF16), ((0, pad), (0, 0))))}
        else:
            i = l // 2
            ffn = {"moe": (jnp.pad(moe_router[i], ((0, 0), (0, V7X_LANES - N_EXPERTS))),
                           moe_w_gate[i].astype(_BF16), moe_w_up[i].astype(_BF16), moe_w_down[i].astype(_BF16))}
        mod = _ada(cond, ada_w, ada_b, l)
        mods_lat = [mod[:bsz, j * d:(j + 1) * d].reshape(bsz, 1, d) for j in range(N_MOD)]
        mods_ctx = [mod[bsz:bsz + 1, j * d:(j + 1) * d].reshape(1, 1, d) for j in range(N_MOD)]
        last = l == DEPTH - 1
        ctx_new, ctx_fin = _stream_mixer(ctx, mods_ctx, lw, lp, l, 1, None, not last)
        if not last:
            ctx = _stream_ffn(ctx_new, mods_ctx, norm2_w[l], l, ffn)
        x, _ = _stream_mixer(x, mods_lat, lw, lp, l, rows, ctx_fin, True)
        fuse_final = last and l % 2 == 1
        x = _stream_ffn(x, mods_lat, norm2_w[l], l, ffn, final_norm_w if fuse_final else None)
    if fuse_final:
        return x
    return _final_norm(x.reshape(bsz * L, d), final_norm_w).reshape(bsz, L, d)
```

```python
import functools
import math

import jax
import jax.numpy as jnp
from jax import lax
from jax.experimental import pallas as pl
from jax.experimental.pallas import tpu as pltpu

D_MODEL = 4096
DEPTH = 2
GRID_W = 64
EPS = 1e-6
SHORT_CONV = 3
N_MOD = 6
NEG_BIG = -1e30
LB_FLOOR = 1e-30

HG_WIDTH = D_MODEL // 4
HG_HEADS = 8
ML_WIDTH = D_MODEL // 2
ML_HEADS = 8
ML_DH = ML_WIDTH // ML_HEADS
HY_WIDTH = D_MODEL // 4
HY_ORDER = 2
HY_POS_DIM = 33
HY_BANDS = (HY_POS_DIM - 1) // 2
HY_MIN_DECAY = math.log(1e-2) / 1.5
HY_MAX_DECAY = math.log(1e-2) / 0.3
N_BRANCH = 3

N_GATE_COLS = 4 * ML_HEADS
N_STATE_MAIN = 3 * HG_WIDTH + 2 * ML_WIDTH
N_STATE_COLS = N_STATE_MAIN + N_GATE_COLS
N_MAIN = N_STATE_MAIN + 2 * HG_WIDTH + 2 * ML_WIDTH + 3 * HY_WIDTH + N_BRANCH * D_MODEL

DENSE_FF = 11008
N_EXPERTS = 8
TOP_K = 2
EXPERT_FF = 2048

V7X_LANES = 128
V7X_VMEM_BYTES = 64 * 1024 * 1024
DENSE_FF_PAD = 11264

_F32 = jnp.float32
_BF16 = jnp.bfloat16


def _vmem_limit(*block_bytes):
    need = int(sum(block_bytes)) + (4 << 20)
    return min(max(need, 16 << 20), V7X_VMEM_BYTES - (6 << 20))


def _mm_kernel(*refs, nk, has_res):
    if has_res:
        x_ref, w_ref, res_ref, gate_ref = refs[:4]
        rest = refs[4:]
    else:
        x_ref, w_ref = refs[:2]
        rest = refs[2:]
    o_ref = rest[0]

    def finish(acc):
        if has_res:
            acc = res_ref[...] + gate_ref[0] * acc
        o_ref[...] = acc.astype(o_ref.dtype)

    part = jnp.dot(x_ref[...], w_ref[...], preferred_element_type=_F32)
    if nk == 1:
        finish(part)
        return
    acc_ref = rest[1]
    k = pl.program_id(2)

    @pl.when(k == 0)
    def _():
        acc_ref[...] = part

    @pl.when(k > 0)
    def _():
        acc_ref[...] += part

    @pl.when(k == nk - 1)
    def _():
        finish(acc_ref[...])


def _matmul(x, w, *, out_dtype, bm=1024, bn=1024, bk=None, res=None, gate=None, rows_per_gate=None):
    m, kdim = x.shape
    n = w.shape[1]
    bm, bn = min(bm, m), min(bn, n)
    bk = kdim if bk is None else bk
    assert m % bm == 0 and n % bn == 0 and kdim % bk == 0, (x.shape, w.shape, bm, bn, bk)
    nk = kdim // bk
    has_res = res is not None
    in_specs = [pl.BlockSpec((bm, bk), lambda i, j, k: (i, k)),
                pl.BlockSpec((bk, bn), lambda i, j, k: (k, j))]
    args = [x, w]
    osz = jnp.dtype(out_dtype).itemsize
    vm = [2 * bm * bk * 2, 2 * bk * bn * 2, 2 * bm * bn * osz, 2 * bm * bn * 4]
    if has_res:
        blocks_per_gate = rows_per_gate // bm
        assert rows_per_gate % bm == 0
        in_specs += [pl.BlockSpec((bm, bn), lambda i, j, k: (i, j)),
                     pl.BlockSpec((1, 1, bn), lambda i, j, k: (i // blocks_per_gate, 0, j))]
        args += [res, gate]
        vm.append(2 * bm * bn * 4)
    scratch = [pltpu.VMEM((bm, bn), _F32)] if nk > 1 else []
    return pl.pallas_call(
        functools.partial(_mm_kernel, nk=nk, has_res=has_res),
        grid=(m // bm, n // bn, nk),
        in_specs=in_specs,
        out_specs=pl.BlockSpec((bm, bn), lambda i, j, k: (i, j)),
        out_shape=jax.ShapeDtypeStruct((m, n), out_dtype),
        scratch_shapes=scratch,
        compiler_params=pltpu.CompilerParams(
            dimension_semantics=("parallel", "parallel", "arbitrary"),
            vmem_limit_bytes=_vmem_limit(*vm)),
        name="matmul",
    )(*args)


def _swiglu_kernel(x_ref, wg_ref, wu_ref, o_ref):
    x = x_ref[...]
    g = jnp.dot(x, wg_ref[...], preferred_element_type=_F32)
    u = jnp.dot(x, wu_ref[...], preferred_element_type=_F32)
    o_ref[...] = (g * jax.nn.sigmoid(g) * u).astype(o_ref.dtype)


def _swiglu(x, wg, wu, *, bm=1024, bn=512):
    m, kdim = x.shape
    n = wg.shape[1]
    bm = min(bm, m)
    assert m % bm == 0 and n % bn == 0
    return pl.pallas_call(
        _swiglu_kernel,
        grid=(m // bm, n // bn),
        in_specs=[pl.BlockSpec((bm, kdim), lambda i, j: (i, 0)),
                  pl.BlockSpec((kdim, bn), lambda i, j: (0, j)),
                  pl.BlockSpec((kdim, bn), lambda i, j: (0, j))],
        out_specs=pl.BlockSpec((bm, bn), lambda i, j: (i, j)),
        out_shape=jax.ShapeDtypeStruct((m, n), _BF16),
        compiler_params=pltpu.CompilerParams(
            dimension_semantics=("parallel", "parallel"),
            vmem_limit_bytes=_vmem_limit(2 * bm * kdim * 2, 4 * kdim * bn * 2, 2 * bm * bn * 2, 4 * bm * bn * 4)),
        name="swiglu",
    )(x, wg, wu)


MOE_TILE = 512


def _route_plan(route, bm):
    m = route.shape[0]
    e_flat = route[:, N_EXPERTS:N_EXPERTS + TOP_K].astype(jnp.int32).T.reshape(-1)
    p_flat = route[:, N_EXPERTS + TOP_K:N_EXPERTS + 2 * TOP_K].T.reshape(-1)
    onehot = (e_flat[:, None] == jnp.arange(N_EXPERTS)[None, :]).astype(jnp.int32)
    csum = jnp.cumsum(onehot, axis=0)
    rank = jnp.sum((csum - onehot) * onehot, axis=1)
    padded = (csum[-1] + bm - 1) // bm * bm
    ends = jnp.cumsum(padded)
    dest = ((ends - padded)[e_flat] + rank).astype(jnp.int32)
    n_rows = TOP_K * m + N_EXPERTS * bm
    src = jnp.zeros((n_rows,), jnp.int32).at[dest].set(jnp.tile(jnp.arange(m, dtype=jnp.int32), TOP_K))
    wrow = jnp.zeros((n_rows,), _F32).at[dest].set(p_flat)
    tile_start = jnp.arange(n_rows // bm, dtype=jnp.int32) * bm
    tile_e = jnp.minimum(jnp.sum(tile_start[:, None] >= ends[None, :], axis=1), N_EXPERTS - 1).astype(jnp.int32)
    n_used = (ends[-1] // bm).astype(jnp.int32).reshape(1)
    return src, wrow, tile_e, n_used, dest


def _gather_kernel(idx_ref, x_hbm, o_ref, sem, *, bm):
    base = pl.program_id(0) * bm

    def row_copy(r, src_row):
        return pltpu.make_async_copy(x_hbm.at[pl.ds(src_row, 1)], o_ref.at[pl.ds(r, 1)], sem)

    def issue(r, c):
        row_copy(r, idx_ref[base + r]).start()
        return c

    def drain(r, c):
        row_copy(r, 0).wait()
        return c

    lax.fori_loop(0, bm, issue, 0, unroll=8)
    lax.fori_loop(0, bm, drain, 0, unroll=8)


def _gather_rows(x, idx, bm=256):
    n = idx.shape[0]
    d = x.shape[1]
    assert n % bm == 0
    return pl.pallas_call(
        functools.partial(_gather_kernel, bm=bm),
        grid_spec=pltpu.PrefetchScalarGridSpec(
            num_scalar_prefetch=1, grid=(n // bm,),
            in_specs=[pl.BlockSpec(memory_space=pl.ANY)],
            out_specs=pl.BlockSpec((bm, d), lambda i, idx_ref: (i, 0)),
            scratch_shapes=[pltpu.SemaphoreType.DMA(())]),
        out_shape=jax.ShapeDtypeStruct((n, d), x.dtype),
        compiler_params=pltpu.CompilerParams(dimension_semantics=("arbitrary",)),
        name="gather_rows",
    )(idx, x)


def _gswiglu_kernel(te_ref, nu_ref, x_ref, wg_ref, wu_ref, w_ref, o_ref):
    @pl.when(pl.program_id(1) < nu_ref[0])
    def _():
        x = x_ref[...].astype(_BF16)
        g = jnp.dot(x, wg_ref[...], preferred_element_type=_F32)
        u = jnp.dot(x, wu_ref[...], preferred_element_type=_F32)
        o_ref[...] = (g * jax.nn.sigmoid(g) * u * w_ref[:, 0:1]).astype(o_ref.dtype)

    @pl.when(pl.program_id(1) >= nu_ref[0])
    def _():
        o_ref[...] = jnp.zeros(o_ref.shape, o_ref.dtype)


def _swiglu_grouped(xs, wg, wu, wrow, tile_e, n_used, bm, bn=512):
    n, d = xs.shape
    ff = wg.shape[2]
    w_spec = pl.BlockSpec((None, d, bn), lambda j, i, te, nu: (te[i], 0, j))
    return pl.pallas_call(
        _gswiglu_kernel,
        grid_spec=pltpu.PrefetchScalarGridSpec(
            num_scalar_prefetch=2, grid=(ff // bn, n // bm),
            in_specs=[pl.BlockSpec((bm, d), lambda j, i, te, nu: (i, 0)), w_spec, w_spec,
                      pl.BlockSpec((bm, V7X_LANES), lambda j, i, te, nu: (i, 0))],
            out_specs=pl.BlockSpec((bm, bn), lambda j, i, te, nu: (i, j))),
        out_shape=jax.ShapeDtypeStruct((n, ff), _BF16),
        compiler_params=pltpu.CompilerParams(
            dimension_semantics=("parallel", "arbitrary"),
            vmem_limit_bytes=_vmem_limit(2 * bm * d * 4, bm * d * 2, 4 * d * bn * 2, 2 * bm * bn * 2, 4 * bm * bn * 4)),
        name="swiglu_grouped",
    )(tile_e, n_used, xs, wg, wu, wrow)


def _gdown_kernel(te_ref, nu_ref, h_ref, w_ref, o_ref):
    @pl.when(pl.program_id(1) < nu_ref[0])
    def _():
        o_ref[...] = jnp.dot(h_ref[...], w_ref[...], preferred_element_type=_F32)

    @pl.when(pl.program_id(1) >= nu_ref[0])
    def _():
        o_ref[...] = jnp.zeros(o_ref.shape, o_ref.dtype)


def _down_grouped(hid, wd, tile_e, n_used, bm, bn=1024):
    n, ff = hid.shape
    d = wd.shape[2]
    return pl.pallas_call(
        _gdown_kernel,
        grid_spec=pltpu.PrefetchScalarGridSpec(
            num_scalar_prefetch=2, grid=(d // bn, n // bm),
            in_specs=[pl.BlockSpec((bm, ff), lambda j, i, te, nu: (i, 0)),
                      pl.BlockSpec((None, ff, bn), lambda j, i, te, nu: (te[i], 0, j))],
            out_specs=pl.BlockSpec((bm, bn), lambda j, i, te, nu: (i, j))),
        out_shape=jax.ShapeDtypeStruct((n, d), _F32),
        compiler_params=pltpu.CompilerParams(
            dimension_semantics=("parallel", "arbitrary"),
            vmem_limit_bytes=_vmem_limit(2 * bm * ff * 2, 2 * ff * bn * 2, 3 * bm * bn * 4)),
        name="down_grouped",
    )(tile_e, n_used, hid, wd)


def _combine_kernel(d_ref, y_hbm, x_ref, g_ref, nw_ref, o_ref, buf, sem, *, bt, m, final_norm):
    base = pl.program_id(0) * bt

    def row_copy(k, r, src_row):
        return pltpu.make_async_copy(y_hbm.at[pl.ds(src_row, 1)], buf.at[k, pl.ds(r, 1)], sem)

    def issue(r, c):
        for k in range(TOP_K):
            row_copy(k, r, d_ref[k * m + base + r]).start()
        return c

    def drain(r, c):
        for k in range(TOP_K):
            row_copy(k, r, 0).wait()
        return c

    lax.fori_loop(0, bt, issue, 0, unroll=8)
    lax.fori_loop(0, bt, drain, 0, unroll=8)
    y = x_ref[...] + g_ref[0] * (buf[0] + buf[1])
    if final_norm:
        y = y * lax.rsqrt(jnp.mean(y * y, axis=-1, keepdims=True) + EPS) * nw_ref[...]
    o_ref[...] = y


def _combine(ys, dest, x, gate, rows_per_gate, norm_w=None, bt=256):
    m, d = x.shape
    assert m % bt == 0 and rows_per_gate % bt == 0
    per = rows_per_gate // bt
    final_norm = norm_w is not None
    nw = norm_w.reshape(1, d) if final_norm else jnp.ones((1, d), _F32)
    return pl.pallas_call(
        functools.partial(_combine_kernel, bt=bt, m=m, final_norm=final_norm),
        grid_spec=pltpu.PrefetchScalarGridSpec(
            num_scalar_prefetch=1, grid=(m // bt,),
            in_specs=[pl.BlockSpec(memory_space=pl.ANY),
                      pl.BlockSpec((bt, d), lambda i, dr: (i, 0)),
                      pl.BlockSpec((1, 1, d), lambda i, dr: (i // per, 0, 0)),
                      pl.BlockSpec((1, d), lambda i, dr: (0, 0))],
            out_specs=pl.BlockSpec((bt, d), lambda i, dr: (i, 0)),
            scratch_shapes=[pltpu.VMEM((TOP_K, bt, d), _F32), pltpu.SemaphoreType.DMA(())]),
        out_shape=jax.ShapeDtypeStruct((m, d), _F32),
        compiler_params=pltpu.CompilerParams(
            dimension_semantics=("arbitrary",),
            vmem_limit_bytes=_vmem_limit(TOP_K * bt * d * 4, 4 * bt * d * 4, 2 * bt * d * 4)),
        name="moe_combine",
    )(dest, ys, x, gate, nw)


def _moe(h, route, x, gate, rows_per_gate, wg, wu, wd, norm_w=None):
    src, wrow, tile_e, n_used, dest = _route_plan(route, MOE_TILE)
    xs = _gather_rows(h, src)
    wrow = jnp.broadcast_to(wrow[:, None], (wrow.shape[0], V7X_LANES))
    hid = _swiglu_grouped(xs, wg, wu, wrow, tile_e, n_used, MOE_TILE)
    ys = _down_grouped(hid, wd, tile_e, n_used, MOE_TILE)
    return _combine(ys, dest, x, gate, rows_per_gate, norm_w)


def _merge_kernel(yhg_ref, yml_ref, yhy_ref, w0_ref, w1_ref, w2_ref, w3_ref, m0_ref, m1_ref, m2_ref, o_ref):
    half = yml_ref.shape[1] // 2
    a_hg = jnp.dot(yhg_ref[...], w0_ref[...], preferred_element_type=_F32)
    a_ml = jnp.dot(yml_ref[:, :half], w1_ref[...], preferred_element_type=_F32) \
        + jnp.dot(yml_ref[:, half:], w2_ref[...], preferred_element_type=_F32)
    a_hy = jnp.dot(yhy_ref[...], w3_ref[...], preferred_element_type=_F32)
    sig = lambda r: jax.nn.sigmoid(r[...].astype(_F32))
    o_ref[...] = (sig(m0_ref) * a_hg + sig(m1_ref) * a_ml + sig(m2_ref) * a_hy).astype(o_ref.dtype)


def _merge(y_hg, y_ml, y_hy, wb, p2d, merge_off, bm=1024, bn=512):
    m = y_hg.shape[0]
    d = wb.shape[1]
    bm = min(bm, m)
    kb = HG_WIDTH
    assert m % bm == 0 and d % bn == 0 and ML_WIDTH == 2 * kb and HY_WIDTH == kb and merge_off % bn == 0
    y_spec = lambda w: pl.BlockSpec((bm, w), lambda i, j: (i, 0))
    w_spec = lambda r: pl.BlockSpec((kb, bn), lambda i, j, r=r: (r, j))
    g_spec = lambda b: pl.BlockSpec((bm, bn), lambda i, j, b=b: (i, (merge_off + b * d) // bn + j))
    return pl.pallas_call(
        _merge_kernel,
        grid=(m // bm, d // bn),
        in_specs=[y_spec(HG_WIDTH), y_spec(ML_WIDTH), y_spec(HY_WIDTH)] + [w_spec(r) for r in range(4)]
        + [g_spec(b) for b in range(N_BRANCH)],
        out_specs=pl.BlockSpec((bm, bn), lambda i, j: (i, j)),
        out_shape=jax.ShapeDtypeStruct((m, d), _BF16),
        compiler_params=pltpu.CompilerParams(
            dimension_semantics=("parallel", "parallel"),
            vmem_limit_bytes=_vmem_limit(2 * bm * 4 * kb * 2, 8 * kb * bn * 2, 6 * bm * bn * 2, 2 * bm * bn * 2,
                                         4 * bm * bn * 4)),
        name="merge",
    )(y_hg, y_ml, y_hy, wb, wb, wb, wb, p2d, p2d, p2d)


def _ada_kernel(c_ref, w_ref, b_ref, o_ref):
    c = c_ref[...]
    a = (c * jax.nn.sigmoid(c)).astype(_BF16)
    o_ref[...] = jnp.dot(a, w_ref[...].astype(_BF16), preferred_element_type=_F32) + b_ref[0]


def _ada(cond, ada_w, ada_b, layer, bn=512):
    rows = cond.shape[0]
    n = N_MOD * D_MODEL
    return pl.pallas_call(
        _ada_kernel,
        grid=(n // bn,),
        in_specs=[pl.BlockSpec((rows, D_MODEL), lambda j: (0, 0)),
                  pl.BlockSpec((None, D_MODEL, bn), lambda j: (layer, 0, j)),
                  pl.BlockSpec((None, 1, bn), lambda j: (layer, 0, j))],
        out_specs=pl.BlockSpec((rows, bn), lambda j: (0, j)),
        out_shape=jax.ShapeDtypeStruct((rows, n), _F32),
        compiler_params=pltpu.CompilerParams(
            dimension_semantics=("parallel",),
            vmem_limit_bytes=_vmem_limit(2 * D_MODEL * bn * 4, D_MODEL * bn * 2)),
        name="ada",
    )(cond, ada_w, ada_b.reshape(DEPTH, 1, n))


def _modnorm_kernel(*refs, with_router):
    if with_router:
        x_ref, w_ref, sh_ref, sc_ref, r_ref, h_ref, comb_ref = refs
    else:
        x_ref, w_ref, sh_ref, sc_ref, h_ref = refs
    x = x_ref[0]
    y = x * lax.rsqrt(jnp.mean(x * x, axis=-1, keepdims=True) + EPS)
    h = (y * w_ref[0]) * (1.0 + sc_ref[0]) + sh_ref[0]
    h_ref[...] = h.astype(h_ref.dtype)
    if with_router:
        logits = jnp.dot(h, r_ref[...], preferred_element_type=_F32, precision=lax.Precision.HIGHEST)
        lane = lax.broadcasted_iota(jnp.int32, logits.shape, 1)
        valid = lane < N_EXPERTS
        lg = jnp.where(valid, logits, -jnp.inf)
        v1 = jnp.max(lg, axis=-1, keepdims=True)
        i1 = jnp.min(jnp.where(lg == v1, lane, V7X_LANES), axis=-1, keepdims=True)
        lg2 = jnp.where(lane == i1, -jnp.inf, lg)
        v2 = jnp.max(lg2, axis=-1, keepdims=True)
        i2 = jnp.min(jnp.where(lg2 == v2, lane, V7X_LANES), axis=-1, keepdims=True)
        e2 = jnp.exp(v2 - v1)
        p1 = 1.0 / (1.0 + e2)
        p2 = e2 / (1.0 + e2)
        comb_ref[...] = (jnp.where(lane == N_EXPERTS, i1.astype(_F32), 0.0)
                         + jnp.where(lane == N_EXPERTS + 1, i2.astype(_F32), 0.0)
                         + jnp.where(lane == N_EXPERTS + 2, p1, 0.0) + jnp.where(lane == N_EXPERTS + 3, p2, 0.0))


def _modnorm(x, norm_w, shift, scale, router=None, bl=256):
    b, L, d = x.shape
    bl = min(bl, L)
    assert L % bl == 0
    nl = L // bl
    per_batch = shift.shape[0] == b and b > 1
    mod_map = (lambda i, j: (i, 0, 0)) if per_batch else (lambda i, j: (0, 0, 0))
    in_specs = [pl.BlockSpec((1, bl, d), lambda i, j: (i, j, 0)),
                pl.BlockSpec((1, d), lambda i, j: (0, 0)),
                pl.BlockSpec((1, 1, d), mod_map),
                pl.BlockSpec((1, 1, d), mod_map)]
    args = [x, norm_w.reshape(1, d), shift, scale]
    out_specs = [pl.BlockSpec((bl, d), lambda i, j: (i * nl + j, 0))]
    with_router = router is not None
    out_shape = [jax.ShapeDtypeStruct((b * L, d), _F32 if with_router else _BF16)]
    if with_router:
        in_specs.append(pl.BlockSpec((d, V7X_LANES), lambda i, j: (0, 0)))
        args.append(router)
        out_specs.append(pl.BlockSpec((bl, V7X_LANES), lambda i, j: (i * nl + j, 0)))
        out_shape.append(jax.ShapeDtypeStruct((b * L, V7X_LANES), _F32))
    out = pl.pallas_call(
        functools.partial(_modnorm_kernel, with_router=with_router),
        grid=(b, nl),
        in_specs=in_specs,
        out_specs=out_specs,
        out_shape=out_shape,
        compiler_params=pltpu.CompilerParams(
            dimension_semantics=("parallel", "parallel"),
            vmem_limit_bytes=_vmem_limit(2 * bl * d * 4, 2 * bl * d * 2, 4 * bl * d * 4, 2 * d * V7X_LANES * 4)),
        name="modnorm",
    )(*args)
    return out if with_router else out[0]


def _final_norm_kernel(x_ref, w_ref, o_ref):
    x = x_ref[...]
    o_ref[...] = x * lax.rsqrt(jnp.mean(x * x, axis=-1, keepdims=True) + EPS) * w_ref[...]


def _final_norm(x, w, bl=256):
    m, d = x.shape
    return pl.pallas_call(
        _final_norm_kernel,
        grid=(m // bl,),
        in_specs=[pl.BlockSpec((bl, d), lambda i: (i, 0)), pl.BlockSpec((1, d), lambda i: (0, 0))],
        out_specs=pl.BlockSpec((bl, d), lambda i: (i, 0)),
        out_shape=jax.ShapeDtypeStruct((m, d), _F32),
        compiler_params=pltpu.CompilerParams(dimension_semantics=("parallel",),
                                             vmem_limit_bytes=_vmem_limit(6 * bl * d * 4)),
        name="final_norm",
    )(x, w.reshape(1, d))


_O_HF, _O_HB, _O_HGI, _O_MLK, _O_MLV = 0, 1024, 2048, 3072, 5120
_O_HGQ, _O_HGOG, _O_MLQ, _O_MLOG, _O_HY, _O_MERGE = 7168, 8192, 9216, 11264, 13312, 16384

HG_DK = HG_WIDTH // HG_HEADS
HG_CHUNK = 64
HG_SUB = 16
HG_SPAN_MAX = 60.0
HG_BLOCK = 256
ML_CHUNK = 256


def _log_sigmoid(x):
    return jnp.minimum(x, 0.0) - jnp.log(1.0 + jnp.exp(-jnp.abs(x)))


def _silu(x):
    return x * jax.nn.sigmoid(x)


def _logaddexp(a, b):
    return jnp.maximum(a, b) + jnp.log(1.0 + jnp.exp(-jnp.abs(a - b)))


def _hgrn2_kernel(*refs, rev, with_out, finalize, has_init, n_chunks):
    refs = list(refs)
    pre_ref, v_ref = refs.pop(0), refs.pop(0)
    q_ref = refs.pop(0) if with_out else None
    lbc_ref = refs.pop(0)
    s0_ref = refs.pop(0) if has_init else None
    if finalize:
        ofwd_ref, og_ref, nw_ref = refs.pop(0), refs.pop(0), refs.pop(0)
    o_ref = refs.pop(0) if with_out else None
    st_ref = refs.pop(0)
    st_sc = refs.pop(0) if with_out else None
    C, SUB = HG_CHUNK, HG_SUB
    n_sub = C // SUB

    @pl.when(pl.program_id(2) == 0)
    def _():
        st_ref[0, 0] = s0_ref[0, 0] if has_init else jnp.zeros(st_ref.shape[2:], _F32)

    lbc = lbc_ref[...]
    log_lb, log_1mlb, one_m_lb = lbc[0:1], lbc[1:2], lbc[2:3]
    r_i = lax.broadcasted_iota(jnp.int32, (C, C), 0)
    c_i = lax.broadcasted_iota(jnp.int32, (C, C), 1)
    tri = ((c_i >= r_i) if rev else (c_i <= r_i)).astype(_F32)
    row64 = lax.broadcasted_iota(jnp.int32, (C, 1), 0)
    row16 = lax.broadcasted_iota(jnp.int32, (SUB, 1), 0)
    lane64 = lax.broadcasted_iota(jnp.int32, (SUB, C), 1)
    last = 0 if rev else C - 1

    tb = n_chunks * C
    rb = lax.broadcasted_iota(jnp.int32, (tb, tb), 0)
    cb = lax.broadcasted_iota(jnp.int32, (tb, tb), 1)
    tri_blk = (((rb // C) == (cb // C)) & ((cb >= rb) if rev else (cb <= rb))).astype(_BF16)

    def block_gates():
        pre = pre_ref[0].astype(_F32)
        g = _logaddexp(log_lb, log_1mlb + _log_sigmoid(pre))
        kk = one_m_lb * jax.nn.sigmoid(-pre)
        g_hi = g.astype(_BF16)
        r1 = g - g_hi.astype(_F32)
        g_mid = r1.astype(_BF16)
        g_lo = (r1 - g_mid.astype(_F32)).astype(_BF16)
        b3 = jnp.dot(tri_blk, jnp.concatenate([g_hi, g_mid, g_lo], axis=1), preferred_element_type=_F32)
        b = b3[:, :HG_DK] + b3[:, HG_DK:2 * HG_DK] + b3[:, 2 * HG_DK:]
        return g, kk, b

    def chunk_terms(c, gates):
        rows = slice(c * C, (c + 1) * C)
        g, kk, b = (a[rows] for a in gates)
        v = v_ref[0, rows, :]
        b_end = b[last:last + 1]
        kt = (kk * jnp.exp(b_end - b)).astype(_BF16)
        upd = lax.dot_general(v, kt, (((0,), (0,)), ((), ())), preferred_element_type=_F32)
        if not with_out:
            return jnp.exp(b_end), upd, None, None
        q = _silu(q_ref[0, rows, :].astype(_F32))
        qe = (q * jnp.exp(b)).astype(_BF16)
        return jnp.exp(b_end), upd, qe, (q, kk, b, b - g, v)

    def pivot(bx, i):
        first = i * SUB + (SUB - 1 if rev else 0)
        return bx[first:first + 1]

    def intra_exact(q, kk, b, bx):
        atts = []
        for i in range(n_sub):
            sub = slice(i * SUB, (i + 1) * SUB)
            piv = pivot(bx, i)
            qi = q[sub] * jnp.exp(b[sub] - piv)
            earlier = (row64 >= (i + 1) * SUB) if rev else (row64 < i * SUB)
            khat = jnp.where(earlier, kk * jnp.exp(jnp.where(earlier, piv - b, 0.0)), 0.0)
            att = lax.dot_general(qi.astype(_BF16), khat.astype(_BF16),
                                  (((1,), (1,)), ((), ())), preferred_element_type=_F32)
            for s in range(SUB):
                r = i * SUB + s
                valid = (row16 <= s) if rev else (row16 >= s)
                rel = jnp.where(valid, b[sub] - b[r:r + 1], 0.0)
                z = jnp.where(valid, q[sub] * jnp.exp(rel) * kk[r:r + 1], 0.0)
                att = jnp.where(lane64 == r, jnp.sum(z, axis=1, keepdims=True), att)
            atts.append(att)
        return jnp.concatenate(atts, axis=0)

    def intra_factored(q, kk, b, bx):
        piv_rows = jnp.concatenate([jnp.broadcast_to(pivot(bx, i), (SUB, HG_DK)) for i in range(n_sub)], axis=0)
        qhat = q * jnp.exp(b - piv_rows)
        lhs, rhs = [], []
        for i in range(n_sub):
            in_sub = (row64 >= i * SUB) & (row64 < (i + 1) * SUB)
            lhs.append(jnp.where(in_sub, qhat, 0.0).astype(_BF16))
            upto = (row64 >= i * SUB) if rev else (row64 < (i + 1) * SUB)
            e = jnp.minimum(jnp.where(upto, pivot(bx, i) - b, 0.0), HG_SPAN_MAX)
            rhs.append(jnp.where(upto, kk * jnp.exp(e), 0.0).astype(_BF16))
        att = lax.dot_general(jnp.concatenate(lhs, axis=1), jnp.concatenate(rhs, axis=1),
                              (((1,), (1,)), ((), ())), preferred_element_type=_F32)
        return jnp.where(tri > 0.0, att, 0.0)

    def write_out(outs):
        o = jnp.concatenate(outs, axis=0)
        if finalize:
            y = ofwd_ref[0] + o
            y = y * lax.rsqrt(jnp.mean(y * y, axis=-1, keepdims=True) + EPS) * nw_ref[...]
            o_ref[0] = (y * _silu(og_ref[0].astype(_F32))).astype(o_ref.dtype)
        else:
            o_ref[0] = o

    def chunk_out(intra, term, st):
        _, _, qe, (q, kk, b, bx, v) = term
        return jnp.dot(intra(q, kk, b, bx).astype(_BF16), v, preferred_element_type=_F32) \
            + lax.dot_general(qe, st.astype(_BF16), (((1,), (1,)), ((), ())), preferred_element_type=_F32)

    gates = block_gates()
    terms = [chunk_terms(c, gates) for c in range(n_chunks)]
    st = st_ref[0, 0]
    outs = [None] * n_chunks
    for c in (reversed(range(n_chunks)) if rev else range(n_chunks)):
        if with_out:
            st_sc[c] = st
            outs[c] = chunk_out(intra_factored, terms[c], st)
        st = st * terms[c][0] + terms[c][1]
    st_ref[0, 0] = st
    if not with_out:
        return
    write_out(outs)
    spans = []
    for term in terms:
        b, bx = term[3][2], term[3][3]
        for i in range(n_sub):
            last_i = i * SUB + (0 if rev else SUB - 1)
            spans.append(pivot(bx, i) - b[last_i:last_i + 1])
    span = jnp.max(jnp.concatenate(spans, axis=0))

    @pl.when(span > HG_SPAN_MAX)
    def _():
        write_out([chunk_out(intra_exact, chunk_terms(c, gates), st_sc[c]) for c in range(n_chunks)])


def _hgrn2(p, lbc, *, d, with_out, init=None, o_fwd=None, norm_w=None):
    bsz, L, _ = p.shape
    rev = d == 1
    tb = min(L, HG_BLOCK)
    assert L % tb == 0 and tb % HG_CHUNK == 0
    nblk = L // tb
    finalize = o_fwd is not None
    has_init = init is not None
    blk = (lambda n: nblk - 1 - n) if rev else (lambda n: n)
    pre_off = (_O_HB if rev else _O_HF) // HG_DK

    def tok_spec(off):
        return pl.BlockSpec((1, tb, HG_DK), lambda b, h, n: (b, blk(n), off + h))

    in_specs = [tok_spec(pre_off), tok_spec(_O_HGI // HG_DK)]
    args = [p, p]
    if with_out:
        in_specs.append(tok_spec(_O_HGQ // HG_DK))
        args.append(p)
    in_specs.append(pl.BlockSpec((None, 8, HG_DK), lambda b, h, n: (d, 0, h)))
    args.append(lbc)
    state_spec = pl.BlockSpec((1, 1, HG_DK, HG_DK), lambda b, h, n: (b, h, 0, 0))
    if has_init:
        in_specs.append(state_spec)
        args.append(init)
    if finalize:
        in_specs += [tok_spec(0), tok_spec(_O_HGOG // HG_DK), pl.BlockSpec((1, HG_DK), lambda b, h, n: (0, h))]
        args += [o_fwd, p, norm_w.reshape(1, HG_WIDTH)]
    out_specs, out_shape = [], []
    if with_out:
        out_specs.append(tok_spec(0))
        out_shape.append(jax.ShapeDtypeStruct((bsz, L, HG_WIDTH), _BF16 if finalize else _F32))
    out_specs.append(state_spec)
    out_shape.append(jax.ShapeDtypeStruct((bsz, HG_HEADS, HG_DK, HG_DK), _F32))
    out = pl.pallas_call(
        functools.partial(_hgrn2_kernel, rev=rev, with_out=with_out, finalize=finalize, has_init=has_init,
                          n_chunks=tb // HG_CHUNK),
        grid=(bsz, HG_HEADS, nblk),
        in_specs=in_specs, out_specs=out_specs, out_shape=out_shape,
        scratch_shapes=[pltpu.VMEM((tb // HG_CHUNK, HG_DK, HG_DK), _F32)] if with_out else [],
        compiler_params=pltpu.CompilerParams(dimension_semantics=("parallel", "parallel", "arbitrary")),
        name="hgrn2",
    )(*args)
    return (out[0], out[1]) if with_out else (None, out[0])


def _mlstm_kernel(*refs, rev, with_out, finalize, has_init, seg, pre_conv, emit_qk):
    refs = list(refs)
    k_ref, v_ref = refs.pop(0), refs.pop(0)
    q_ref = refs.pop(0) if with_out else None
    g_ref = refs.pop(0)
    if not pre_conv:
        ckw_ref, ckb_ref = refs.pop(0), refs.pop(0)
        if with_out:
            cqw_ref, cqb_ref = refs.pop(0), refs.pop(0)
    if has_init:
        c0_ref, n0_ref, m0_ref = refs.pop(0), refs.pop(0), refs.pop(0)
    if finalize:
        ofwd_ref, og_ref, nw_ref = refs.pop(0), refs.pop(0), refs.pop(0)
    o_ref = refs.pop(0) if with_out else None
    if emit_qk:
        kc_ref, qc_ref = refs.pop(0), refs.pop(0)
    c_ref, n_ref, m_ref = refs
    C = k_ref.shape[1]

    @pl.when(pl.program_id(2) == 0)
    def _():
        if has_init:
            c_ref[...] = c0_ref[...]
            n_ref[...] = n0_ref[...]
            m_ref[...] = m0_ref[...]
        else:
            c_ref[...] = jnp.zeros(c_ref.shape, _F32)
            n_ref[...] = jnp.zeros(n_ref.shape, _F32)
            m_ref[...] = jnp.zeros(m_ref.shape, _F32)

    g = g_ref[0, 0]
    ji = 2 if rev else 0
    i_row = g[ji:ji + 1]
    f_row = _log_sigmoid(g[ji + 1:ji + 2])
    r_i = lax.broadcasted_iota(jnp.int32, (C, C), 0)
    c_i = lax.broadcasted_iota(jnp.int32, (C, C), 1)
    mask = (c_i >= r_i) if rev else (c_i <= r_i)
    upto = ((r_i >= c_i) if rev else (r_i <= c_i)).astype(_F32)
    b_row = jnp.dot(jnp.broadcast_to(f_row, (8, C)), upto, preferred_element_type=_F32,
                    precision=lax.Precision.HIGHEST)[0:1]
    b_col = jnp.sum(jnp.where(mask, f_row, 0.0), axis=1, keepdims=True)
    i_col = jnp.sum(jnp.where(r_i == c_i, i_row, 0.0), axis=1, keepdims=True)
    b_end = jnp.sum(f_row, axis=1, keepdims=True)
    m_prev = m_ref[0, 0][0:1, 0:1]
    cm = c_ref[0, 0]
    nrow = n_ref[0, 0]

    pos = lax.broadcasted_iota(jnp.int32, (C, 1), 0) % seg

    def conv(x_ref, w_ref, b_ref):
        x = x_ref[0].astype(_F32)
        w = w_ref[...]
        x_prev = jnp.where(pos == 0, 0.0, pltpu.roll(x, 1, 0))
        x_next = jnp.where(pos == seg - 1, 0.0, pltpu.roll(x, C - 1, 0))
        return b_ref[...] + w[0:1] * x_prev + w[1:2] * x + w[2:3] * x_next

    k = k_ref[0].astype(_F32) if pre_conv else _silu(conv(k_ref, ckw_ref, ckb_ref)) * ML_DH ** -0.5
    v = v_ref[0]

    if with_out:
        q = q_ref[0].astype(_F32) if pre_conv else _silu(conv(q_ref, cqw_ref, cqb_ref))
        if emit_qk:
            kc_ref[0] = k.astype(kc_ref.dtype)
            qc_ref[0] = q.astype(qc_ref.dtype)
        qb = q.astype(_BF16)
        logd = b_col - b_row + i_row
        m_t = jnp.maximum(b_col + m_prev, jnp.max(jnp.where(mask, logd, NEG_BIG), axis=1, keepdims=True))
        prev = jnp.exp(b_col + m_prev - m_t)
        dmat = jnp.where(mask, jnp.exp(jnp.where(mask, logd - m_t, 0.0)), 0.0)
        s = lax.dot_general(qb, k.astype(_BF16), (((1,), (1,)), ((), ())), preferred_element_type=_F32) * dmat
        num = prev * jnp.dot(qb, cm.astype(_BF16), preferred_element_type=_F32) \
            + jnp.dot(s.astype(_BF16), v, preferred_element_type=_F32)
        den = prev * jnp.sum(q * nrow, axis=1, keepdims=True) + jnp.sum(s, axis=1, keepdims=True)
        h = num / jnp.maximum(jnp.abs(den), jnp.exp(-m_t))
        if finalize:
            y = ofwd_ref[0] + h
            y = y * lax.rsqrt(jnp.mean(y * y, axis=-1, keepdims=True) + EPS) * nw_ref[...]
            o_ref[0] = (y * jax.nn.sigmoid(og_ref[0].astype(_F32))).astype(o_ref.dtype)
        else:
            o_ref[0] = h

    log_w = b_end - b_col + i_col
    m_new = jnp.maximum(b_end + m_prev, jnp.max(log_w, axis=0, keepdims=True))
    decay = jnp.exp(b_end + m_prev - m_new)
    wk = jnp.exp(log_w - m_new) * k
    c_ref[0, 0] = decay * cm + lax.dot_general(wk.astype(_BF16), v, (((0,), (0,)), ((), ())),
                                               preferred_element_type=_F32)
    n_ref[0, 0] = decay * nrow + jnp.sum(wk, axis=0, keepdims=True)
    m_ref[0, 0] = jnp.broadcast_to(m_new, m_ref.shape[2:])


def _mlstm(p, gate_rows, conv_w, conv_b, layer, *, d, seg, with_out, init=None, o_fwd=None, norm_w=None, qk=None,
           emit_qk=False):
    bsz, L, _ = p.shape
    rev = d == 1
    C = min(L, ML_CHUNK)
    assert L % C == 0 and C % seg == 0
    nch = L // C
    finalize = o_fwd is not None
    has_init = init is not None
    pre_conv = qk is not None
    assert with_out or not (pre_conv or emit_qk)
    blk = (lambda n: nch - 1 - n) if rev else (lambda n: n)

    def tok_spec(off):
        return pl.BlockSpec((1, C, ML_DH), lambda b, h, n: (b, blk(n), off + h))

    def conv_specs(off):
        return [pl.BlockSpec((None, SHORT_CONV, ML_DH), lambda b, h, n: (layer, 0, off + h)),
                pl.BlockSpec((None, 1, ML_DH), lambda b, h, n: (layer, 0, off + h))]

    conv_b3 = conv_b.reshape(DEPTH, 1, 2 * ML_WIDTH)
    in_specs = [tok_spec(0 if pre_conv else _O_MLK // ML_DH), tok_spec(_O_MLV // ML_DH)]
    args = [qk[0] if pre_conv else p, p]
    if with_out:
        in_specs.append(tok_spec(0 if pre_conv else _O_MLQ // ML_DH))
        args.append(qk[1] if pre_conv else p)
    in_specs.append(pl.BlockSpec((1, 1, 8, C), lambda b, h, n: (b, h, 0, blk(n))))
    args.append(gate_rows)
    if not pre_conv:
        in_specs += conv_specs(0)
        args += [conv_w, conv_b3]
        if with_out:
            in_specs += conv_specs(ML_HEADS)
            args += [conv_w, conv_b3]
    state_specs = [pl.BlockSpec((1, 1, ML_DH, ML_DH), lambda b, h, n: (b, h, 0, 0)),
                   pl.BlockSpec((1, 1, 1, ML_DH), lambda b, h, n: (b, h, 0, 0)),
                   pl.BlockSpec((1, 1, 8, V7X_LANES), lambda b, h, n: (b, h, 0, 0))]
    state_shape = [jax.ShapeDtypeStruct((bsz, ML_HEADS, ML_DH, ML_DH), _F32),
                   jax.ShapeDtypeStruct((bsz, ML_HEADS, 1, ML_DH), _F32),
                   jax.ShapeDtypeStruct((bsz, ML_HEADS, 8, V7X_LANES), _F32)]
    if has_init:
        in_specs += state_specs
        args += list(init)
    if finalize:
        in_specs += [tok_spec(0), tok_spec(_O_MLOG // ML_DH), pl.BlockSpec((1, ML_DH), lambda b, h, n: (0, h))]
        args += [o_fwd, p, norm_w.reshape(1, ML_WIDTH)]
    out_specs, out_shape = [], []
    if with_out:
        out_specs.append(tok_spec(0))
        out_shape.append(jax.ShapeDtypeStruct((bsz, L, ML_WIDTH), _BF16 if finalize else _F32))
    if emit_qk:
        out_specs += [tok_spec(0), tok_spec(0)]
        out_shape += [jax.ShapeDtypeStruct((bsz, L, ML_WIDTH), _BF16)] * 2
    out = pl.pallas_call(
        functools.partial(_mlstm_kernel, rev=rev, with_out=with_out, finalize=finalize, has_init=has_init, seg=seg,
                          pre_conv=pre_conv, emit_qk=emit_qk),
        grid=(bsz, ML_HEADS, nch),
        in_specs=in_specs, out_specs=out_specs + state_specs, out_shape=out_shape + state_shape,
        compiler_params=pltpu.CompilerParams(dimension_semantics=("parallel", "parallel", "arbitrary")),
        name="mlstm",
    )(*args)
    if emit_qk:
        return out[0], tuple(out[3:]), (out[1], out[2])
    return (out[0], tuple(out[1:])) if with_out else (None, tuple(out))


def _dft_kernel(c_ref, s_ref, st_ref, *, n_fft, bm):
    shape = c_ref.shape
    row = lax.broadcasted_iota(jnp.int32, shape, 0) + pl.program_id(0) * bm
    col = lax.broadcasted_iota(jnp.int32, shape, 1)
    ang = ((row * col) & (n_fft - 1)).astype(_F32) * (2.0 * math.pi / n_fft)
    alt_col = jnp.where((col & 1) == 0, 1.0, -1.0)
    alt_row = jnp.where((row & 1) == 0, 1.0, -1.0)
    msin = -jnp.sin(ang)
    c_ref[...] = jnp.cos(ang).astype(c_ref.dtype)
    s_ref[...] = jnp.where(row == 0, alt_col, msin).astype(s_ref.dtype)
    st_ref[...] = jnp.where(col == 0, alt_row, msin).astype(st_ref.dtype)


def _dft_mats(L, bm=256):
    bm = min(bm, L)
    spec = pl.BlockSpec((bm, L), lambda i: (i, 0))
    shape = jax.ShapeDtypeStruct((L, L), _BF16)
    return pl.pallas_call(
        functools.partial(_dft_kernel, n_fft=2 * L, bm=bm),
        grid=(L // bm,), in_specs=[], out_specs=[spec] * 3, out_shape=[shape] * 3,
        compiler_params=pltpu.CompilerParams(dimension_semantics=("parallel",)),
        name="dft_mats",
    )()


def _hy_filter_kernel(band_ref, w1t_ref, w1c_ref, w1s_ref, b1_ref, w2_ref, b2_ref, w3_ref, fr_ref, dl_ref,
                      klo_ref, khi_ref, asum_ref, *, L, bl):
    hp = lax.Precision.HIGHEST
    m = (lax.broadcasted_iota(jnp.int32, (bl, 1), 0) + pl.program_id(0) * bl)
    half = HY_ORDER * HY_WIDTH
    total = jnp.zeros((1, half), _F32)
    for side, out_ref in ((0, klo_ref), (1, khi_ref)):
        pos = (m if side == 0 else (L - 1 - m)).astype(_F32)
        t = pos / (L - 1)
        ang = (2.0 * math.pi / L) * pos * band_ref[...]
        a = t * w1t_ref[...] + jnp.dot(jnp.cos(ang), w1c_ref[...], precision=hp, preferred_element_type=_F32) \
            + jnp.dot(jnp.sin(ang), w1s_ref[...], precision=hp, preferred_element_type=_F32) + b1_ref[...]
        a = jnp.sin(fr_ref[0:1] * a)
        a = jnp.sin(fr_ref[1:2] * (jnp.dot(a, w2_ref[...], precision=hp, preferred_element_type=_F32) + b2_ref[...]))
        f = jnp.dot(a, w3_ref[:, side * half:(side + 1) * half], precision=hp, preferred_element_type=_F32)
        f = f * jnp.exp(-t * dl_ref[...])
        if side == 1:
            f = jnp.where(m == 0, 0.0, f)
        out_ref[...] = f.astype(out_ref.dtype)
        total = total + jnp.sum(jnp.abs(f), axis=0, keepdims=True)

    @pl.when(pl.program_id(0) == 0)
    def _():
        asum_ref[...] = jnp.zeros(asum_ref.shape, _F32)

    asum_ref[...] += jnp.broadcast_to(total, asum_ref.shape)


def _hy_filter(L, w1, b1, w2, b2, w3, freq, bl=256):
    bl = min(bl, L)
    ffn = w1.shape[1]
    half = HY_ORDER * HY_WIDTH
    bands = jnp.pad(jnp.linspace(1e-4, HY_BANDS - 1, HY_BANDS, dtype=_F32), (0, V7X_LANES - HY_BANDS)).reshape(1, -1)
    pad_rows = ((0, V7X_LANES - HY_BANDS), (0, 0))
    deltas = jnp.abs(jnp.linspace(HY_MIN_DECAY, HY_MAX_DECAY, HY_WIDTH, dtype=_F32))
    args = [bands, w1[0:1], jnp.pad(w1[1:1 + HY_BANDS], pad_rows), jnp.pad(w1[1 + HY_BANDS:], pad_rows),
            b1.reshape(1, ffn), w2, b2.reshape(1, ffn), w3, freq, jnp.tile(deltas, HY_ORDER).reshape(1, half)]
    full = lambda a: pl.BlockSpec(a.shape, lambda i: (0,) * a.ndim)
    return pl.pallas_call(
        functools.partial(_hy_filter_kernel, L=L, bl=bl),
        grid=(L // bl,),
        in_specs=[full(a) for a in args],
        out_specs=[pl.BlockSpec((bl, half), lambda i: (i, 0)), pl.BlockSpec((bl, half), lambda i: (i, 0)),
                   pl.BlockSpec((8, half), lambda i: (0, 0))],
        out_shape=[jax.ShapeDtypeStruct((L, half), _BF16), jax.ShapeDtypeStruct((L, half), _BF16),
                   jax.ShapeDtypeStruct((8, half), _F32)],
        compiler_params=pltpu.CompilerParams(dimension_semantics=("arbitrary",)),
        name="hy_filter",
    )(*args)


def _hy_kf_kernel(t_ref, klo_ref, khi_ref, asum_ref, o_ref, *, bm):
    row = lax.broadcasted_iota(jnp.int32, (bm, 1), 0) + pl.program_id(0) * bm
    sgn = jnp.where((row & 1) == 0, 1.0, -1.0)
    t = t_ref[...]
    acc = jnp.dot(t, klo_ref[...], preferred_element_type=_F32) \
        + sgn * jnp.dot(t, khi_ref[...], preferred_element_type=_F32)
    o_ref[...] = acc / (asum_ref[0:1] + EPS)


def _hy_kf(cmat, smat, klo, khi, asum, bm=512, bn=512):
    L = cmat.shape[0]
    tmat = jnp.concatenate([cmat, smat], axis=0)
    half = klo.shape[1]
    bm, bn = min(bm, L), min(bn, half)
    return pl.pallas_call(
        functools.partial(_hy_kf_kernel, bm=bm),
        grid=(2 * L // bm, half // bn),
        in_specs=[pl.BlockSpec((bm, L), lambda i, j: (i, 0)),
                  pl.BlockSpec((L, bn), lambda i, j: (0, j)), pl.BlockSpec((L, bn), lambda i, j: (0, j)),
                  pl.BlockSpec((8, bn), lambda i, j: (0, j))],
        out_specs=pl.BlockSpec((bm, bn), lambda i, j: (i, j)),
        out_shape=jax.ShapeDtypeStruct((2 * L, half), _F32),
        compiler_params=pltpu.CompilerParams(
            dimension_semantics=("parallel", "parallel"),
            vmem_limit_bytes=_vmem_limit(2 * bm * L * 2, 4 * L * bn * 2, 6 * bm * bn * 4)),
        name="hy_kf",
    )(tmat, klo, khi, asum)


def _hy_split_kernel(*refs, seg):
    p_refs, w_refs, b_refs, o_refs = refs[0:3], refs[3:6], refs[6:9], refs[9:12]
    bl = o_refs[0].shape[1]
    pos = lax.broadcasted_iota(jnp.int32, (bl, 1), 0) % seg
    for p_ref, w_ref, b_ref, o_ref in zip(p_refs, w_refs, b_refs, o_refs):
        x = p_ref[0].astype(_F32)
        w = w_ref[...]
        x_prev = jnp.where(pos == 0, 0.0, pltpu.roll(x, 1, 0))
        x_next = jnp.where(pos == seg - 1, 0.0, pltpu.roll(x, bl - 1, 0))
        o_ref[0] = (b_ref[...] + w[0:1] * x_prev + w[1:2] * x + w[2:3] * x_next).astype(o_ref.dtype)


def _hy_split(p, conv_w, conv_b, seg):
    bsz, L, _ = p.shape
    bl = min(L, 256)
    assert L % bl == 0 and bl % seg == 0
    base = _O_HY // HY_WIDTH
    out_spec = pl.BlockSpec((1, bl, HY_WIDTH), lambda b, i: (b, i, 0))
    conv_b2 = conv_b.reshape(1, 3 * HY_WIDTH)
    in_specs = [pl.BlockSpec((1, bl, HY_WIDTH), lambda b, i, j=j: (b, i, base + j)) for j in range(3)]
    in_specs += [pl.BlockSpec((SHORT_CONV, HY_WIDTH), lambda b, i, j=j: (0, j)) for j in range(3)]
    in_specs += [pl.BlockSpec((1, HY_WIDTH), lambda b, i, j=j: (0, j)) for j in range(3)]
    return pl.pallas_call(
        functools.partial(_hy_split_kernel, seg=seg),
        grid=(bsz, L // bl),
        in_specs=in_specs,
        out_specs=[out_spec] * 3,
        out_shape=[jax.ShapeDtypeStruct((bsz, L, HY_WIDTH), _BF16)] * 3,
        compiler_params=pltpu.CompilerParams(dimension_semantics=("parallel", "parallel")),
        name="hy_split",
    )(p, p, p, conv_w, conv_w, conv_w, conv_b2, conv_b2, conv_b2)


def _hy_fwd_kernel(c_ref, s_ref, z_ref, kre_ref, kim_ref, pre_ref, pim_ref, *, bm):
    z = z_ref[...]
    xre = jnp.dot(c_ref[...], z, preferred_element_type=_F32)
    xim = jnp.dot(s_ref[...], z, preferred_element_type=_F32)
    kre, kim = kre_ref[...], kim_ref[...]
    row0 = (lax.broadcasted_iota(jnp.int32, (bm, 1), 0) + pl.program_id(0) * bm) == 0
    pre = jnp.where(row0, 0.5 * xre * kre, xre * kre - xim * kim)
    pim = jnp.where(row0, 0.5 * xim * kim, xre * kim + xim * kre)
    pre_ref[...] = pre.astype(pre_ref.dtype)
    pim_ref[...] = pim.astype(pim_ref.dtype)


def _hy_fwd(cmat, smat, z, kf, order, bm=512, bn=512):
    bsz, L, width = z.shape
    bm, bn = min(bm, L), min(bn, width)
    nj = width // bn
    mat_spec = pl.BlockSpec((bm, L), lambda i, b, j: (i, 0))
    out_spec = pl.BlockSpec((None, bm, bn), lambda i, b, j: (b, i, j))
    out_shape = jax.ShapeDtypeStruct((bsz, L, width), _BF16)
    return pl.pallas_call(
        functools.partial(_hy_fwd_kernel, bm=bm),
        grid=(L // bm, bsz, nj),
        in_specs=[mat_spec, mat_spec,
                  pl.BlockSpec((None, L, bn), lambda i, b, j: (b, 0, j)),
                  pl.BlockSpec((bm, bn), lambda i, b, j: (i, order * nj + j)),
                  pl.BlockSpec((bm, bn), lambda i, b, j: (L // bm + i, order * nj + j))],
        out_specs=[out_spec, out_spec], out_shape=[out_shape, out_shape],
        compiler_params=pltpu.CompilerParams(
            dimension_semantics=("parallel", "parallel", "parallel"),
            vmem_limit_bytes=_vmem_limit(4 * bm * L * 2, 2 * L * bn * 2, 4 * bm * bn * 4, 4 * bm * bn * 2,
                                         4 * bm * bn * 4)),
        name="hy_fwd",
    )(cmat, smat, z, kf, kf)


def _hy_inv_kernel(c_ref, st_ref, pre_ref, pim_ref, z_ref, g_ref, skip_ref, o_ref, *, scale):
    conv = jnp.dot(c_ref[...], pre_ref[...], preferred_element_type=_F32) \
        + jnp.dot(st_ref[...], pim_ref[...], preferred_element_type=_F32)
    z = z_ref[...].astype(_F32)
    o_ref[...] = (g_ref[...].astype(_F32) * (scale * conv + skip_ref[...] * z)).astype(o_ref.dtype)


def _hy_inv(cmat, smat_t, pre, pim, z, gate, skip, bm=512, bn=512):
    bsz, L, width = z.shape
    bm, bn = min(bm, L), min(bn, width)
    mat_spec = pl.BlockSpec((bm, L), lambda i, b, j: (i, 0))
    spec_spec = pl.BlockSpec((None, L, bn), lambda i, b, j: (b, 0, j))
    tok_spec = pl.BlockSpec((None, bm, bn), lambda i, b, j: (b, i, j))
    return pl.pallas_call(
        functools.partial(_hy_inv_kernel, scale=1.0 / L),
        grid=(L // bm, bsz, width // bn),
        in_specs=[mat_spec, mat_spec, spec_spec, spec_spec, tok_spec, tok_spec,
                  pl.BlockSpec((1, bn), lambda i, b, j: (0, j))],
        out_specs=tok_spec,
        out_shape=jax.ShapeDtypeStruct((bsz, L, width), _BF16),
        compiler_params=pltpu.CompilerParams(
            dimension_semantics=("parallel", "parallel", "parallel"),
            vmem_limit_bytes=_vmem_limit(4 * bm * L * 2, 4 * L * bn * 2, 6 * bm * bn * 2, 3 * bm * bn * 4)),
        name="hy_inv",
    )(cmat, smat_t, pre, pim, z, gate, skip.reshape(1, width))


def _hyena(p, lp, seg, dft):
    L = p.shape[1]
    cmat, smat, smat_t = dft
    klo, khi, asum = _hy_filter(L, lp["hy_w1"], lp["hy_b1"], lp["hy_w2"], lp["hy_b2"], lp["hy_w3"], lp["hy_freq"])
    kf = _hy_kf(cmat, smat, klo, khi, asum)
    z, x1, x2 = _hy_split(p, lp["hy_conv_w"], lp["hy_conv_b"], seg)
    for o, gate in enumerate((x1, x2)):
        pre, pim = _hy_fwd(cmat, smat, z, kf, o)
        z = _hy_inv(cmat, smat_t, pre, pim, z, gate, lp["hy_skip"][o])
    return z


def _prep_layer_weights(l, w_in, w_branch, w_out):
    wl = w_in[l]
    w_main = jnp.concatenate([wl[:, :N_STATE_MAIN], wl[:, N_STATE_COLS:]], axis=1).astype(_BF16)
    w_gate = jnp.pad(wl[:, N_STATE_MAIN:N_STATE_COLS], ((0, 0), (0, V7X_LANES - N_GATE_COLS))).astype(_BF16)
    return w_main, w_gate, w_branch[l].astype(_BF16), w_out[l].astype(_BF16)


def _lower_bound_rows(lb):
    lb = lb.astype(_F32)
    rows = jnp.stack([jnp.log(jnp.maximum(lb, LB_FLOOR)), jnp.log1p(-lb), 1.0 - lb], axis=1)
    return jnp.pad(rows, ((0, 0), (0, 5), (0, 0)))


def _gate_rows(gates, gate_b):
    bsz, L, _ = gates.shape
    g = (gates[..., :N_GATE_COLS] + gate_b).reshape(bsz, L, 4, ML_HEADS)
    return jnp.pad(g.transpose(0, 3, 2, 1), ((0, 0), (0, 0), (0, 4), (0, 0)))


def _stream_mixer(xs, mods, lw, lp, layer, n_rows, init, with_out):
    w_main, w_gate, wb, wo = lw
    bsz, L, d = xs.shape
    sh, sc, g = mods[0], mods[1], mods[2]
    h = _modnorm(xs, lp["norm1"], sh, sc)
    w_proj = w_main if with_out else w_main[:, :N_STATE_MAIN]
    p = _matmul(h, w_proj, out_dtype=_BF16).reshape(bsz, L, -1)
    gates = _matmul(h, w_gate, out_dtype=_F32).reshape(bsz, L, V7X_LANES)
    grows = _gate_rows(gates, lp["ml_gate_b"])
    seg = L // n_rows
    hg_init, ml_init = init if init is not None else ((None, None), (None, None))
    hg_kw = dict(with_out=with_out)
    ml_kw = dict(seg=seg, with_out=with_out)
    o_hg, s_hg_f = _hgrn2(p, lp["lbc"], d=0, init=hg_init[0], **hg_kw)
    y_hg, s_hg_b = _hgrn2(p, lp["lbc"], d=1, init=hg_init[1], o_fwd=o_hg, norm_w=lp["hg_norm_w"], **hg_kw)
    o_ml, s_ml_f, *qk = _mlstm(p, grows, lp["ml_conv_w"], lp["ml_conv_b"], layer, d=0, init=ml_init[0],
                               emit_qk=with_out, **ml_kw)
    y_ml, s_ml_b = _mlstm(p, grows, lp["ml_conv_w"], lp["ml_conv_b"], layer, d=1, init=ml_init[1],
                          o_fwd=o_ml, norm_w=lp["ml_norm_w"], qk=qk[0] if qk else None, **ml_kw)
    finals = ((s_hg_f, s_hg_b), (s_ml_f, s_ml_b))
    if not with_out:
        return None, finals
    y_hy = _hyena(p, lp, seg, lp["dft"][L])
    m = bsz * L
    merged = _merge(y_hg.reshape(m, -1), y_ml.reshape(m, -1), y_hy.reshape(m, -1), wb, p.reshape(m, -1), _O_MERGE)
    x_new = _matmul(merged, wo, out_dtype=_F32, res=xs.reshape(m, d), gate=g,
                    rows_per_gate=L if g.shape[0] > 1 else m)
    return x_new.reshape(bsz, L, d), finals


def _stream_ffn(xs, mods, norm2, l, ffn, final_w=None):
    bsz, L, d = xs.shape
    m = bsz * L
    sh2, sc2, g2 = mods[3], mods[4], mods[5]
    rows_per_gate = L if g2.shape[0] > 1 else m
    if l % 2 == 0:
        wg, wu, wd = ffn["dense"]
        h = _modnorm(xs, norm2, sh2, sc2)
        hid = _swiglu(h, wg, wu)
        out = _matmul(hid, wd, out_dtype=_F32, bk=DENSE_FF_PAD // 4, res=xs.reshape(m, d), gate=g2,
                      rows_per_gate=rows_per_gate)
    else:
        router, wg, wu, wd = ffn["moe"]
        h, route = _modnorm(xs, norm2, sh2, sc2, router=router)
        out = _moe(h, route, xs.reshape(m, d), g2, rows_per_gate, wg, wu, wd, final_w)
    return out.reshape(bsz, L, d)


def kernel(x, c, ctx, c_ctx, norm1_w, norm2_w, ada_w, ada_b, w_in, hg_lb_logits, hg_norm_w,
           ml_conv_w, ml_conv_b, ml_gate_b, ml_norm_w, hy_conv_w, hy_conv_b, hy_w1, hy_b1,
           hy_w2, hy_b2, hy_w3, hy_freq, hy_skip, w_branch, w_out, dense_w_gate, dense_w_up,
           dense_w_down, moe_router, moe_w_gate, moe_w_up, moe_w_down, final_norm_w):
    bsz, L, d = x.shape
    rows = L // GRID_W
    lb_w = jax.nn.softmax(hg_lb_logits.astype(_F32), axis=0)
    lower_bounds = jnp.cumsum(lb_w, axis=0) - lb_w[0]

    cond = jnp.zeros((8, d), _F32).at[:bsz].set(c).at[bsz].set(c_ctx)
    dft = {n: _dft_mats(n) for n in {L, ctx.shape[1]}}
    for l in range(DEPTH):
        lp = {
            "norm1": norm1_w[l], "hg_norm_w": hg_norm_w[l], "lbc": _lower_bound_rows(lower_bounds[l]), "dft": dft,
            "ml_conv_w": ml_conv_w, "ml_conv_b": ml_conv_b, "ml_gate_b": ml_gate_b[l],
            "ml_norm_w": ml_norm_w[l], "hy_conv_w": hy_conv_w[l], "hy_conv_b": hy_conv_b[l],
            "hy_w1": hy_w1[l], "hy_b1": hy_b1[l], "hy_w2": hy_w2[l], "hy_b2": hy_b2[l],
            "hy_w3": hy_w3[l], "hy_freq": hy_freq[l], "hy_skip": hy_skip[l],
        }
        lw = _prep_layer_weights(l, w_in, w_branch, w_out)
        if l % 2 == 0:
            i = l // 2
            pad = DENSE_FF_PAD - DENSE_FF
            ffn = {"dense": (jnp.pad(dense_w_gate[i].astype(_BF16), ((0, 0), (0, pad))),
                             jnp.pad(dense_w_up[i].astype(_BF16), ((0, 0), (0, pad))),
                             jnp.pad(dense_w_down[i].astype(_BF16), ((0, pad), (0, 0))))}
        else:
            i = l // 2
            ffn = {"moe": (jnp.pad(moe_router[i], ((0, 0), (0, V7X_LANES - N_EXPERTS))),
                           moe_w_gate[i].astype(_BF16), moe_w_up[i].astype(_BF16), moe_w_down[i].astype(_BF16))}
        mod = _ada(cond, ada_w, ada_b, l)
        mods_lat = [mod[:bsz, j * d:(j + 1) * d].reshape(bsz, 1, d) for j in range(N_MOD)]
        mods_ctx = [mod[bsz:bsz + 1, j * d:(j + 1) * d].reshape(1, 1, d) for j in range(N_MOD)]
        last = l == DEPTH - 1
        ctx_new, ctx_fin = _stream_mixer(ctx, mods_ctx, lw, lp, l, 1, None, not last)
        if not last:
            ctx = _stream_ffn(ctx_new, mods_ctx, norm2_w[l], l, ffn)
        x, _ = _stream_mixer(x, mods_lat, lw, lp, l, rows, ctx_fin, True)
        fuse_final = last and l % 2 == 1
        x = _stream_ffn(x, mods_lat, norm2_w[l], l, ffn, final_norm_w if fuse_final else None)
    if fuse_final:
        return x
    return _final_norm(x.reshape(bsz * L, d), final_norm_w).reshape(bsz, L, d)
```
